```python
import numpy as np
import jax
import jax.numpy as jnp
from jax import lax

D_MODEL = 1024
BATCH = 4
SEQ = 8192
DEPTH = 2
DEC_BATCH = 128
DEC_SEQ = 1
PAST_LEN = 16384
PAGE_SIZE = 128

HEAD_DIM = 64
FOX_HEADS = 8
FOX_KV_HEADS = 4
FOX_Q_PER_KV = FOX_HEADS // FOX_KV_HEADS
FOX_GATE_BIAS = 2.0
NSA_HEADS = 8
NSA_KV_HEADS = 2
NSA_Q_PER_KV = NSA_HEADS // NSA_KV_HEADS
NSA_CMP_BLOCK = 32
NSA_SEL_BLOCK = 64
CMP_PER_SEL = NSA_SEL_BLOCK // NSA_CMP_BLOCK
NSA_TOPK = 16
NSA_WINDOW = 512
SWA_HEADS = 16
SWA_KV_HEADS = 2
SWA_Q_PER_KV = SWA_HEADS // SWA_KV_HEADS
SWA_WINDOW = 128
D_FF = ((8 * D_MODEL + 3 * 256 - 1) // (3 * 256)) * 256

Q_BLOCK = 128
N_EVEN = (DEPTH + 1) // 2
N_ODD = DEPTH // 2
EVEN_SIZES = (FOX_HEADS * HEAD_DIM, 2 * FOX_KV_HEADS * HEAD_DIM, FOX_HEADS,
              NSA_HEADS * HEAD_DIM, 6 * NSA_KV_HEADS * HEAD_DIM, 3 * NSA_HEADS)
EVEN_IN = sum(EVEN_SIZES)
EVEN_MIX = FOX_HEADS * HEAD_DIM + NSA_HEADS * HEAD_DIM
ODD_SIZES = (SWA_HEADS * HEAD_DIM, 2 * SWA_KV_HEADS * HEAD_DIM)
ODD_IN = sum(ODD_SIZES)
ODD_MIX = SWA_HEADS * HEAD_DIM
ATTN_SCALE = HEAD_DIM ** -0.5
RMS_EPS = 1e-6
NEG_INF = -1e30
TINY = 1e-30
FORCED_SCORE = 1e6

kernel_name = 'fox_nsa_swa_hybrid_decode_step'


def _rmsnorm(x, g):
    xf = x.astype(jnp.float32)
    y = xf * lax.rsqrt(jnp.mean(xf * xf, axis=-1, keepdims=True) + RMS_EPS)
    return (y * g.astype(jnp.float32)).astype(x.dtype)


def _swiglu(x, w_gate, w_up, w_down):
    return (jax.nn.silu(x @ w_gate) * (x @ w_up)) @ w_down


def _split(z, sizes):
    offsets = [int(v) for v in np.cumsum(sizes)[:-1]]
    return jnp.split(z, offsets, axis=-1)


def _alibi_slopes(n_heads, n_groups):
    exps = np.arange(1, n_heads + 1, dtype=np.float32)
    slopes = np.power(np.float32(2.0), -8.0 * exps / n_heads).astype(np.float32)
    return jnp.asarray(slopes.reshape(n_groups, n_heads // n_groups))


def _masked_probs(s, mask, sink=None):
    s = jnp.where(mask, s, NEG_INF)
    m = jnp.max(s, axis=-1, keepdims=True)
    if sink is not None:
        m = jnp.maximum(m, sink)
    p = jnp.where(mask, jnp.exp(s - m), 0.0)
    den = jnp.sum(p, axis=-1, keepdims=True)
    if sink is not None:
        den = den + jnp.exp(sink - m)
    return p / jnp.maximum(den, TINY)


def _alibi_attend(q, k, v, qpos, kpos, slopes, window=None, sink=None):
    s = jnp.einsum('bqgrd,bkgd->bgrqk', q, k).astype(jnp.float32) * ATTN_SCALE
    dist = qpos[:, None] - kpos[None, :]
    s = s - slopes[:, :, None, None] * dist.astype(jnp.float32)
    mask = (dist >= 0) & (kpos[None, :] >= 0)
    if window is not None:
        mask = mask & (dist <= window)
    sink_b = None if sink is None else sink.astype(jnp.float32)[None, :, :, None, None]
    p = _masked_probs(s, mask, sink_b)
    o = jnp.einsum('bgrqk,bkgd->bqgrd', p.astype(v.dtype), v)
    return o, p


def _fox_attend(q, cq, qpos, segments):
    cq_t = jnp.moveaxis(cq, 1, 3)[..., None]
    scores, masks, sizes = [], [], []
    for k, _, ck, kpos in segments:
        s = jnp.einsum('bqgrd,bkgd->bgrqk', q, k).astype(jnp.float32) * ATTN_SCALE
        scores.append(s + cq_t - jnp.moveaxis(ck, 1, 3)[..., None, :])
        masks.append(qpos[:, None] >= kpos[None, :])
        sizes.append(k.shape[1])
    p = _masked_probs(jnp.concatenate(scores, axis=-1), jnp.concatenate(masks, axis=-1))
    parts = jnp.split(p, [int(o) for o in np.cumsum(sizes)[:-1]], axis=-1)
    out = None
    for p_i, (_, v, _, _) in zip(parts, segments):
        o_i = jnp.einsum('bgrqk,bkgd->bqgrd', p_i.astype(v.dtype), v)
        out = o_i if out is None else out + o_i
    return out


def _nsa_compress(kv_rows, pe_cmp, w_cmp, pos0):
    B, L = kv_rows.shape[:2]
    nc = L // NSA_CMP_BLOCK
    blocks = kv_rows[:, : nc * NSA_CMP_BLOCK].reshape(B, nc, NSA_CMP_BLOCK, 2, NSA_KV_HEADS, HEAD_DIM)
    summ = jnp.einsum('bnigd,igde->bnige', jnp.mean(blocks + pe_cmp, axis=2), w_cmp)
    cpos = pos0 + (jnp.arange(nc) + 1) * NSA_CMP_BLOCK - 1
    return summ[:, :, 0], summ[:, :, 1], cpos


def _nsa_block(q, gates, qpos, k_cmp, v_cmp, cpos, n_keys, gather_sel, k_win, v_win, wpos, slopes):
    B, Q = q.shape[:2]
    o_cmp, p_cmp = _alibi_attend(q, k_cmp, v_cmp, qpos, cpos, slopes)
    n_sel = -(-n_keys // NSA_SEL_BLOCK)
    imp = jnp.sum(p_cmp, axis=2)
    pad = n_sel * CMP_PER_SEL - imp.shape[-1]
    imp = jnp.pad(imp, ((0, 0), (0, 0), (0, 0), (0, pad)))
    imp = imp.reshape(B, NSA_KV_HEADS, Q, n_sel, CMP_PER_SEL).sum(-1)
    blk = jnp.arange(n_sel)[None, :]
    cur = (qpos // NSA_SEL_BLOCK)[:, None]
    visible = blk * NSA_SEL_BLOCK <= qpos[:, None]
    forced = (blk == 0) | (blk == cur) | (blk == cur - 1)
    imp = jnp.where(visible, jnp.where(forced, FORCED_SCORE, imp), -1.0)
    n_top = min(NSA_TOPK, n_sel)
    _, idx = lax.top_k(imp, n_top)
    pos = (idx[..., None] * NSA_SEL_BLOCK + jnp.arange(NSA_SEL_BLOCK)).reshape(B, NSA_KV_HEADS, Q, n_top * NSA_SEL_BLOCK)
    k_sel, v_sel = gather_sel(pos)
    s = jnp.einsum('bqgrd,bgqsd->bgrqs', q, k_sel).astype(jnp.float32) * ATTN_SCALE
    dist = (qpos[None, None, :, None] - pos)[:, :, None]
    s = s - slopes[None, :, :, None, None] * dist.astype(jnp.float32)
    p = _masked_probs(s, dist >= 0)
    o_sel = jnp.einsum('bgrqs,bgqsd->bqgrd', p.astype(v_sel.dtype), v_sel)
    o_win, _ = _alibi_attend(q, k_win, v_win, qpos, wpos, slopes, window=NSA_WINDOW)
    g = jax.nn.sigmoid(gates.astype(jnp.float32)).astype(q.dtype)
    o = g[..., 0:1] * o_cmp + g[..., 1:2] * o_sel + g[..., 2:3] * o_win
    return o.reshape(B, Q, NSA_HEADS * HEAD_DIM)


def _even_project(h, w_in, b_forget):
    B, T, _ = h.shape
    fq, fkv, fl, nq, nkv, ng = _split(h @ w_in, EVEN_SIZES)
    fq = fq.reshape(B, T, FOX_KV_HEADS, FOX_Q_PER_KV, HEAD_DIM)
    fkv = fkv.reshape(B, T, 2, FOX_KV_HEADS, HEAD_DIM)
    logf = jax.nn.log_sigmoid((fl + b_forget).astype(jnp.float32))
    nq = nq.reshape(B, T, NSA_KV_HEADS, NSA_Q_PER_KV, HEAD_DIM)
    nkv = nkv.reshape(B, T, 6, NSA_KV_HEADS, HEAD_DIM)
    ng = ng.reshape(B, T, NSA_KV_HEADS, NSA_Q_PER_KV, 3)
    return fq, fkv, logf, nq, nkv, ng


def _even_mixer_prompt(h, w_in, b_forget, pe_cmp, w_cmp, w_out):
    B, T, _ = h.shape
    fq, fkv, logf, nq, nkv, ng = _even_project(h, w_in, b_forget)
    c = jnp.cumsum(logf, axis=1).reshape(B, T, FOX_KV_HEADS, FOX_Q_PER_KV)
    fk, fv = fkv[:, :, 0], fkv[:, :, 1]
    cmp_rows, sel_rows, win_rows = nkv[:, :, 0:2], nkv[:, :, 2:4], nkv[:, :, 4:6]
    k_cmp, v_cmp, cpos = _nsa_compress(cmp_rows, pe_cmp, w_cmp, 0)
    win_pad = jnp.pad(win_rows, ((0, 0), (NSA_WINDOW, 0), (0, 0), (0, 0), (0, 0)))
    b_idx = jnp.arange(B)[:, None, None, None]
    g_idx = jnp.arange(NSA_KV_HEADS)[None, :, None, None]
    slopes = _alibi_slopes(NSA_HEADS, NSA_KV_HEADS)
    kpos = jnp.arange(T)

    def gather_sel(pos):
        rows = sel_rows[b_idx, jnp.clip(pos, 0, T - 1), :, g_idx]
        return rows[..., 0, :], rows[..., 1, :]

    def block(i):
        s0 = i * Q_BLOCK
        qpos = s0 + jnp.arange(Q_BLOCK)

        def rows(a):
            return lax.dynamic_slice_in_dim(a, s0, Q_BLOCK, axis=1)

        o_fox = _fox_attend(rows(fq), rows(c), qpos, [(fk, fv, c, kpos)])
        wblk = lax.dynamic_slice_in_dim(win_pad, s0, NSA_WINDOW + Q_BLOCK, axis=1)
        wpos = s0 - NSA_WINDOW + jnp.arange(NSA_WINDOW + Q_BLOCK)
        o_nsa = _nsa_block(rows(nq), rows(ng), qpos, k_cmp, v_cmp, cpos, T, gather_sel,
                           wblk[:, :, 0], wblk[:, :, 1], wpos, slopes)
        return jnp.concatenate([o_fox.reshape(B, Q_BLOCK, FOX_HEADS * HEAD_DIM), o_nsa], axis=-1)

    o = lax.map(block, jnp.arange(T // Q_BLOCK))
    o = jnp.moveaxis(o, 0, 1).reshape(B, T, EVEN_MIX)
    new_win = win_rows[:, T - min(NSA_WINDOW, T):]
    return o @ w_out, (fkv, logf, cmp_rows, sel_rows, new_win)


def _even_mixer_sample(h, e, cache_fox_kv, cache_fox_logf, cache_cmp_kv, cache_sel_kv, state_win, page_table,
                       w_in, b_forget, pe_cmp, w_cmp, w_out):
    B, T, _ = h.shape
    fq, fkv, logf, nq, nkv, ng = _even_project(h, w_in, b_forget)
    cmp_rows, sel_rows, win_rows = nkv[:, :, 0:2], nkv[:, :, 2:4], nkv[:, :, 4:6]
    qpos = PAST_LEN + jnp.arange(T)
    kpos_past = jnp.arange(PAST_LEN)
    k_past = cache_fox_kv[e, page_table, :, 0].reshape(B, PAST_LEN, FOX_KV_HEADS, HEAD_DIM)
    v_past = cache_fox_kv[e, page_table, :, 1].reshape(B, PAST_LEN, FOX_KV_HEADS, HEAD_DIM)
    logf_past = cache_fox_logf[e, page_table].reshape(B, PAST_LEN, FOX_HEADS).astype(jnp.float32)
    c_all = jnp.cumsum(jnp.concatenate([logf_past, logf], axis=1), axis=1)
    c_all = c_all.reshape(B, PAST_LEN + T, FOX_KV_HEADS, FOX_Q_PER_KV)
    c_past, c_new = c_all[:, :PAST_LEN], c_all[:, PAST_LEN:]
    o_fox = _fox_attend(fq, c_new, qpos, [(k_past, v_past, c_past, kpos_past),
                                          (fkv[:, :, 0], fkv[:, :, 1], c_new, qpos)])
    cmp_past = cache_cmp_kv[e, page_table].reshape(B, PAST_LEN, 2, NSA_KV_HEADS, HEAD_DIM)
    kc_old, vc_old, cpos_old = _nsa_compress(cmp_past, pe_cmp, w_cmp, 0)
    kc_new, vc_new, cpos_new = _nsa_compress(cmp_rows, pe_cmp, w_cmp, PAST_LEN)
    k_cmp = jnp.concatenate([kc_old, kc_new], axis=1)
    v_cmp = jnp.concatenate([vc_old, vc_new], axis=1)
    cpos = jnp.concatenate([cpos_old, cpos_new])
    b_idx = jnp.arange(B)[:, None, None, None]
    g_idx = jnp.arange(NSA_KV_HEADS)[None, :, None, None]
    slopes = _alibi_slopes(NSA_HEADS, NSA_KV_HEADS)

    def gather_sel(pos):
        p_old = jnp.clip(pos, 0, PAST_LEN - 1)
        phys = page_table[b_idx, p_old // PAGE_SIZE]
        old = cache_sel_kv[e, phys, p_old % PAGE_SIZE, :, g_idx]
        new = sel_rows[b_idx, jnp.clip(pos - PAST_LEN, 0, T - 1), :, g_idx]
        rows = jnp.where((pos >= PAST_LEN)[..., None, None], new, old)
        return rows[..., 0, :], rows[..., 1, :]

    win_all = jnp.concatenate([state_win, win_rows], axis=1)
    n_buf = state_win.shape[1]
    wpos = PAST_LEN - n_buf + jnp.arange(n_buf + T)
    o_nsa = _nsa_block(nq, ng, qpos, k_cmp, v_cmp, cpos, PAST_LEN + T, gather_sel,
                       win_all[:, :, 0], win_all[:, :, 1], wpos, slopes)
    o = jnp.concatenate([o_fox.reshape(B, T, FOX_HEADS * HEAD_DIM), o_nsa], axis=-1)
    return o @ w_out, (fkv, logf, cmp_rows, sel_rows, win_all[:, T:])


def _odd_project(h, w_in):
    B, T, _ = h.shape
    q, kv = _split(h @ w_in, ODD_SIZES)
    q = q.reshape(B, T, SWA_KV_HEADS, SWA_Q_PER_KV, HEAD_DIM)
    kv = kv.reshape(B, T, 2, SWA_KV_HEADS, HEAD_DIM)
    return q, kv


def _odd_mixer_prompt(h, w_in, sinks, w_out):
    B, T, _ = h.shape
    q, kv = _odd_project(h, w_in)
    kv_pad = jnp.pad(kv, ((0, 0), (SWA_WINDOW, 0), (0, 0), (0, 0), (0, 0)))
    sink = sinks.reshape(SWA_KV_HEADS, SWA_Q_PER_KV)
    slopes = _alibi_slopes(SWA_HEADS, SWA_KV_HEADS)

    def block(i):
        s0 = i * Q_BLOCK
        qpos = s0 + jnp.arange(Q_BLOCK)
        q_b = lax.dynamic_slice_in_dim(q, s0, Q_BLOCK, axis=1)
        kvb = lax.dynamic_slice_in_dim(kv_pad, s0, SWA_WINDOW + Q_BLOCK, axis=1)
        kpos = s0 - SWA_WINDOW + jnp.arange(SWA_WINDOW + Q_BLOCK)
        o, _ = _alibi_attend(q_b, kvb[:, :, 0], kvb[:, :, 1], qpos, kpos, slopes, SWA_WINDOW, sink)
        return o.reshape(B, Q_BLOCK, ODD_MIX)

    o = lax.map(block, jnp.arange(T // Q_BLOCK))
    o = jnp.moveaxis(o, 0, 1).reshape(B, T, ODD_MIX)
    return o @ w_out, kv[:, T - min(SWA_WINDOW, T):]


def _odd_mixer_sample(h, state, w_in, sinks, w_out):
    B, T, _ = h.shape
    q, kv = _odd_project(h, w_in)
    kv_all = jnp.concatenate([state, kv], axis=1)
    n_buf = state.shape[1]
    qpos = PAST_LEN + jnp.arange(T)
    kpos = PAST_LEN - n_buf + jnp.arange(n_buf + T)
    sink = sinks.reshape(SWA_KV_HEADS, SWA_Q_PER_KV)
    slopes = _alibi_slopes(SWA_HEADS, SWA_KV_HEADS)
    o, _ = _alibi_attend(q, kv_all[:, :, 0], kv_all[:, :, 1], qpos, kpos, slopes, SWA_WINDOW, sink)
    return o.reshape(B, T, ODD_MIX) @ w_out, kv_all[:, T:]


def setup_inputs(seed: int = 0) -> dict:
    key = jax.random.key(seed)
    keys = iter(jax.random.split(key, 32))

    def nrm(shape, scale=1.0):
        return scale * jax.random.normal(next(keys), shape, dtype=jnp.float32)

    n_pages = PAST_LEN // PAGE_SIZE
    n_phys = (5 * DEC_BATCH * n_pages + 3) // 4
    n_nsa_buf = min(NSA_WINDOW, PAST_LEN)
    n_swa_buf = min(SWA_WINDOW, PAST_LEN)
    perm = jax.random.permutation(next(keys), n_phys)
    page_table = perm[: DEC_BATCH * n_pages].reshape(DEC_BATCH, n_pages).astype(jnp.int32)
    return {
        'x_prompt': nrm((BATCH, SEQ, D_MODEL)),
        'x_sample': nrm((DEC_BATCH, DEC_SEQ, D_MODEL)),
        'cache_fox_kv': nrm((N_EVEN, n_phys, PAGE_SIZE, 2, FOX_KV_HEADS, HEAD_DIM)),
        'cache_fox_logf': jax.nn.log_sigmoid(FOX_GATE_BIAS + nrm((N_EVEN, n_phys, PAGE_SIZE, FOX_HEADS))),
        'cache_nsa_cmp_kv': nrm((N_EVEN, n_phys, PAGE_SIZE, 2, NSA_KV_HEADS, HEAD_DIM)),
        'cache_nsa_sel_kv': nrm((N_EVEN, n_phys, PAGE_SIZE, 2, NSA_KV_HEADS, HEAD_DIM)),
        'state_nsa_win': nrm((N_EVEN, DEC_BATCH, n_nsa_buf, 2, NSA_KV_HEADS, HEAD_DIM)),
        'state_swa': nrm((N_ODD, DEC_BATCH, n_swa_buf, 2, SWA_KV_HEADS, HEAD_DIM)),
        'page_table': page_table,
        'even_w_in': nrm((N_EVEN, D_MODEL, EVEN_IN), D_MODEL ** -0.5),
        'even_b_forget': FOX_GATE_BIAS + nrm((N_EVEN, FOX_HEADS), 0.1),
        'even_pe_cmp': nrm((N_EVEN, NSA_CMP_BLOCK, 2, NSA_KV_HEADS, HEAD_DIM), 0.1),
        'even_w_cmp': nrm((N_EVEN, 2, NSA_KV_HEADS, HEAD_DIM, HEAD_DIM), HEAD_DIM ** -0.5),
        'even_w_out': nrm((N_EVEN, EVEN_MIX, D_MODEL), EVEN_MIX ** -0.5),
        'odd_w_in': nrm((N_ODD, D_MODEL, ODD_IN), D_MODEL ** -0.5),
        'odd_sinks': nrm((N_ODD, SWA_HEADS)),
        'odd_w_out': nrm((N_ODD, ODD_MIX, D_MODEL), ODD_MIX ** -0.5),
        'norm_mix': 1.0 + nrm((DEPTH, D_MODEL), 0.05),
        'norm_ffn': 1.0 + nrm((DEPTH, D_MODEL), 0.05),
        'ffn_w_gate': nrm((DEPTH, D_MODEL, D_FF), D_MODEL ** -0.5),
        'ffn_w_up': nrm((DEPTH, D_MODEL, D_FF), D_MODEL ** -0.5),
        'ffn_w_down': nrm((DEPTH, D_FF, D_MODEL), D_FF ** -0.5),
        'norm_final': 1.0 + nrm((D_MODEL,), 0.05),
    }


def reference(x_prompt, x_sample, cache_fox_kv, cache_fox_logf, cache_nsa_cmp_kv, cache_nsa_sel_kv,
              state_nsa_win, state_swa, page_table, even_w_in, even_b_forget, even_pe_cmp, even_w_cmp,
              even_w_out, odd_w_in, odd_sinks, odd_w_out, norm_mix, norm_ffn, ffn_w_gate, ffn_w_up,
              ffn_w_down, norm_final):
    xp, xs = x_prompt, x_sample
    fkv_p, fkv_s, lf_p, lf_s, cmp_p, cmp_s, sel_p, sel_s, win_p, win_s, swa_p, swa_s = (
        [], [], [], [], [], [], [], [], [], [], [], [])
    for i in range(DEPTH):
        hp = _rmsnorm(xp, norm_mix[i])
        hs = _rmsnorm(xs, norm_mix[i])
        if i % 2 == 0:
            e = i // 2
            yp, (a, b, c, d, f) = _even_mixer_prompt(hp, even_w_in[e], even_b_forget[e], even_pe_cmp[e],
                                                     even_w_cmp[e], even_w_out[e])
            ys, (a2, b2, c2, d2, f2) = _even_mixer_sample(hs, e, cache_fox_kv, cache_fox_logf, cache_nsa_cmp_kv,
                                                          cache_nsa_sel_kv, state_nsa_win[e], page_table,
                                                          even_w_in[e], even_b_forget[e], even_pe_cmp[e],
                                                          even_w_cmp[e], even_w_out[e])
            fkv_p.append(a); lf_p.append(b); cmp_p.append(c); sel_p.append(d); win_p.append(f)
            fkv_s.append(a2); lf_s.append(b2); cmp_s.append(c2); sel_s.append(d2); win_s.append(f2)
        else:
            o = i // 2
            yp, sp = _odd_mixer_prompt(hp, odd_w_in[o], odd_sinks[o], odd_w_out[o])
            ys, ss = _odd_mixer_sample(hs, state_swa[o], odd_w_in[o], odd_sinks[o], odd_w_out[o])
            swa_p.append(sp); swa_s.append(ss)
        xp = xp + yp
        xs = xs + ys
        xp = xp + _swiglu(_rmsnorm(xp, norm_ffn[i]), ffn_w_gate[i], ffn_w_up[i], ffn_w_down[i])
        xs = xs + _swiglu(_rmsnorm(xs, norm_ffn[i]), ffn_w_gate[i], ffn_w_up[i], ffn_w_down[i])
    y_prompt = _rmsnorm(xp, norm_final)
    y_sample = _rmsnorm(xs, norm_final)
    fox_kv_prompt, fox_kv_sample = jnp.stack(fkv_p), jnp.stack(fkv_s)
    fox_logf_prompt, fox_logf_sample = jnp.stack(lf_p), jnp.stack(lf_s)
    nsa_cmp_prompt, nsa_cmp_sample = jnp.stack(cmp_p), jnp.stack(cmp_s)
    nsa_sel_prompt, nsa_sel_sample = jnp.stack(sel_p), jnp.stack(sel_s)
    nsa_win_prompt, nsa_win_sample = jnp.stack(win_p), jnp.stack(win_s)
    swa_prompt, swa_sample = jnp.stack(swa_p), jnp.stack(swa_s)
    return (y_prompt, y_sample, fox_kv_prompt, fox_kv_sample, fox_logf_prompt, fox_logf_sample,
            nsa_cmp_prompt, nsa_cmp_sample, nsa_sel_prompt, nsa_sel_sample, nsa_win_prompt, nsa_win_sample,
            swa_prompt, swa_sample)
```

```python
import functools

import numpy as np
import jax
import jax.numpy as jnp
from jax import lax
from jax.experimental import pallas as pl
from jax.experimental.pallas import tpu as pltpu

F32 = jnp.float32
BF16 = jnp.bfloat16

HEAD_DIM = 64
PAGE_SIZE = 128
FOX_HEADS = 8
FOX_KV_HEADS = 4
NSA_HEADS = 8
NSA_KV_HEADS = 2
NSA_CMP_BLOCK = 32
NSA_SEL_BLOCK = 64
NSA_TOPK = 16
NSA_WINDOW = 512
SWA_HEADS = 16
SWA_KV_HEADS = 2
SWA_WINDOW = 128
ATTN_SCALE = HEAD_DIM ** -0.5
RMS_EPS = 1e-6
NEG_INF = -1e30
TINY = 1e-30
FORCED_SCORE = 1e6

VMEM_LIMIT_BYTES = 56 * 1024 * 1024


def _alibi_slopes(n_heads):
    exps = np.arange(1, n_heads + 1, dtype=np.float32)
    return np.power(np.float32(2.0), -8.0 * exps / n_heads).astype(np.float32)


NSA_SLOPES = [float(v) for v in _alibi_slopes(NSA_HEADS)]
SWA_SLOPES = [float(v) for v in _alibi_slopes(SWA_HEADS)]


def _cparams(*sem):
    return pltpu.CompilerParams(dimension_semantics=sem, vmem_limit_bytes=VMEM_LIMIT_BYTES)


def _dot(a, b):
    return jnp.dot(a, b, preferred_element_type=F32)


def _dot_nt(a, b):
    return lax.dot_general(a, b, (((1,), (1,)), ((), ())), preferred_element_type=F32)


def _split3(x):
    hi = x.astype(BF16)
    r1 = x - hi.astype(F32)
    mid = r1.astype(BF16)
    lo = (r1 - mid.astype(F32)).astype(BF16)
    return hi, mid, lo


def _rms(x, g):
    return x * lax.rsqrt(jnp.mean(x * x, axis=-1, keepdims=True) + RMS_EPS) * g


def _log_sigmoid(x):
    return jnp.minimum(x, 0.0) - jnp.log1p(jnp.exp(-jnp.abs(x)))


def _iota(shape, dim):
    return lax.broadcasted_iota(jnp.int32, shape, dim)


def _lane_pick(x, lane_idx):
    lane = _iota(x.shape, 1)
    return jnp.sum(jnp.where(lane == lane_idx, x, 0.0), axis=-1, keepdims=True)


def _row_to_col(row):
    n = row.shape[-1]
    full = jnp.broadcast_to(row, (n, n))
    return jnp.sum(jnp.where(_iota((n, n), 0) == _iota((n, n), 1), full, 0.0), axis=-1, keepdims=True)


def _even_proj_kernel(*refs, with_rows):
    if with_rows:
        (x_ref, g_ref, wn_ref, wt_ref, bsm_ref, bcol_ref, wr_ref,
         q_ref, sm_ref, fkvT_ref, cmpT_ref, selT_ref, winT_ref, logfT_ref,
         fkvTb_ref, selTb_ref, winTb_ref, rows_ref) = refs
    else:
        (x_ref, g_ref, wn_ref, wt_ref, bsm_ref, bcol_ref,
         q_ref, sm_ref, fkvT_ref, cmpT_ref, selT_ref, winT_ref, logfT_ref,
         fkvTb_ref, selTb_ref, winTb_ref) = refs
    h = _rms(x_ref[0], g_ref[...]).astype(BF16)
    n = _dot(h, wn_ref[...])
    q_ref[0] = n[:, :1024].astype(BF16)
    sm = n[:, 1024:] + bsm_ref[...]
    lane = _iota(sm.shape, 1)
    sm_ref[0] = jnp.where(lane < FOX_HEADS, _log_sigmoid(sm), jax.nn.sigmoid(sm))
    t = _dot_nt(wt_ref[...], h)
    fkv = t[0:512]
    sel = t[768:1024]
    win = t[1024:1280]
    fkvT_ref[0] = fkv
    cmpT_ref[0] = t[512:768]
    selT_ref[0] = sel
    winT_ref[0] = win
    logfT_ref[0] = _log_sigmoid(t[1280:1288] + bcol_ref[...])
    fkvTb_ref[0] = fkv.astype(BF16)
    selTb_ref[0] = sel.astype(BF16)
    winTb_ref[0] = win.astype(BF16)
    if with_rows:
        rows_ref[0] = _dot(h, wr_ref[...])


def _even_proj(x, g, wn, wt, bsm, bcol, wr=None):
    bx, tx, d = x.shape
    tm = min(512, tx)
    with_rows = wr is not None
    const2 = lambda b, j: (0, 0)
    in_specs = [pl.BlockSpec((1, tm, d), lambda b, j: (b, j, 0)),
                pl.BlockSpec((1, d), const2),
                pl.BlockSpec(wn.shape, const2),
                pl.BlockSpec(wt.shape, const2),
                pl.BlockSpec((1, 128), const2),
                pl.BlockSpec((FOX_HEADS, 1), const2)]
    args = [x, g, wn, wt, bsm, bcol]
    tspec = lambda rows: pl.BlockSpec((1, rows, tm), lambda b, j: (b, 0, j))
    out_specs = [pl.BlockSpec((1, tm, 1024), lambda b, j: (b, j, 0)),
                 pl.BlockSpec((1, tm, 128), lambda b, j: (b, j, 0)),
                 tspec(512), tspec(256), tspec(256), tspec(256), tspec(FOX_HEADS),
                 tspec(512), tspec(256), tspec(256)]
    out_shape = [jax.ShapeDtypeStruct((bx, tx, 1024), BF16),
                 jax.ShapeDtypeStruct((bx, tx, 128), F32),
                 jax.ShapeDtypeStruct((bx, 512, tx), F32),
                 jax.ShapeDtypeStruct((bx, 256, tx), F32),
                 jax.ShapeDtypeStruct((bx, 256, tx), F32),
                 jax.ShapeDtypeStruct((bx, 256, tx), F32),
                 jax.ShapeDtypeStruct((bx, FOX_HEADS, tx), F32),
                 jax.ShapeDtypeStruct((bx, 512, tx), BF16),
                 jax.ShapeDtypeStruct((bx, 256, tx), BF16),
                 jax.ShapeDtypeStruct((bx, 256, tx), BF16)]
    if with_rows:
        in_specs.append(pl.BlockSpec(wr.shape, const2))
        args.append(wr)
        out_specs.append(pl.BlockSpec((1, tm, wr.shape[1]), lambda b, j: (b, j, 0)))
        out_shape.append(jax.ShapeDtypeStruct((bx, tx, wr.shape[1]), F32))
    return pl.pallas_call(
        functools.partial(_even_proj_kernel, with_rows=with_rows),
        grid=(bx, tx // tm), in_specs=in_specs, out_specs=out_specs, out_shape=out_shape,
        compiler_params=_cparams("parallel", "parallel"), name="even_proj")(*args)


def _odd_proj_kernel(*refs, with_rows):
    if with_rows:
        x_ref, g_ref, wn_ref, wt_ref, wr_ref, q_ref, kvT_ref, kvTb_ref, rows_ref = refs
    else:
        x_ref, g_ref, wn_ref, wt_ref, q_ref, kvT_ref, kvTb_ref = refs
    h = _rms(x_ref[0], g_ref[...]).astype(BF16)
    q_ref[0] = _dot(h, wn_ref[...]).astype(BF16)
    t = _dot_nt(wt_ref[...], h)
    kvT_ref[0] = t
    kvTb_ref[0] = t.astype(BF16)
    if with_rows:
        rows_ref[0] = _dot(h, wr_ref[...])


def _odd_proj(x, g, wn, wt, wr=None):
    bx, tx, d = x.shape
    tm = min(512, tx)
    with_rows = wr is not None
    const2 = lambda b, j: (0, 0)
    in_specs = [pl.BlockSpec((1, tm, d), lambda b, j: (b, j, 0)),
                pl.BlockSpec((1, d), const2),
                pl.BlockSpec(wn.shape, const2),
                pl.BlockSpec(wt.shape, const2)]
    args = [x, g, wn, wt]
    out_specs = [pl.BlockSpec((1, tm, 1024), lambda b, j: (b, j, 0)),
                 pl.BlockSpec((1, 256, tm), lambda b, j: (b, 0, j)),
                 pl.BlockSpec((1, 256, tm), lambda b, j: (b, 0, j))]
    out_shape = [jax.ShapeDtypeStruct((bx, tx, 1024), BF16),
                 jax.ShapeDtypeStruct((bx, 256, tx), F32),
                 jax.ShapeDtypeStruct((bx, 256, tx), BF16)]
    if with_rows:
        in_specs.append(pl.BlockSpec(wr.shape, const2))
        args.append(wr)
        out_specs.append(pl.BlockSpec((1, tm, 256), lambda b, j: (b, j, 0)))
        out_shape.append(jax.ShapeDtypeStruct((bx, tx, 256), F32))
    return pl.pallas_call(
        functools.partial(_odd_proj_kernel, with_rows=with_rows),
        grid=(bx, tx // tm), in_specs=in_specs, out_specs=out_specs, out_shape=out_shape,
        compiler_params=_cparams("parallel", "parallel"), name="odd_proj")(*args)


def _cumsum_kernel(lT_ref, sm_ref, triu_ref, tril_ref, cT_ref, c_ref, carT_ref, carR_ref):
    @pl.when(pl.program_id(1) == 0)
    def _():
        carT_ref[...] = jnp.zeros_like(carT_ref)
        carR_ref[...] = jnp.zeros_like(carR_ref)

    tm = lT_ref.shape[-1]
    triu = triu_ref[...]
    hi, mid, lo = _split3(lT_ref[0])
    cT = _dot(hi, triu) + _dot(mid, triu) + _dot(lo, triu) + carT_ref[:, 0:1]
    cT_ref[0] = cT
    carT_ref[...] = jnp.broadcast_to(cT[:, tm - 1:tm], carT_ref.shape)
    tril = tril_ref[...]
    hi, mid, lo = _split3(sm_ref[0])
    c = _dot(tril, hi) + _dot(tril, mid) + _dot(tril, lo) + carR_ref[0:1, :]
    c_ref[0] = c
    carR_ref[...] = jnp.broadcast_to(c[tm - 1:tm, :], carR_ref.shape)


def _cumsum(logfT, sm):
    b, _, t = logfT.shape
    tm = min(512, t)
    i = np.arange(tm)
    triu = jnp.asarray(i[:, None] <= i[None, :], BF16)
    tril = jnp.asarray(i[:, None] >= i[None, :], BF16)
    return pl.pallas_call(
        _cumsum_kernel, grid=(b, t // tm),
        in_specs=[pl.BlockSpec((1, FOX_HEADS, tm), lambda b, j: (b, 0, j)),
                  pl.BlockSpec((1, tm, 128), lambda b, j: (b, j, 0)),
                  pl.BlockSpec((tm, tm), lambda b, j: (0, 0)),
                  pl.BlockSpec((tm, tm), lambda b, j: (0, 0))],
        out_specs=[pl.BlockSpec((1, FOX_HEADS, tm), lambda b, j: (b, 0, j)),
                   pl.BlockSpec((1, tm, 128), lambda b, j: (b, j, 0))],
        out_shape=[jax.ShapeDtypeStruct((b, FOX_HEADS, t), F32),
                   jax.ShapeDtypeStruct((b, t, 128), F32)],
        scratch_shapes=[pltpu.VMEM((FOX_HEADS, 128), F32), pltpu.VMEM((8, 128), F32)],
        compiler_params=_cparams("parallel", "arbitrary"), name="fox_cumsum")(logfT, sm, triu, tril)


def _softmax_step(carry, s, vT, mask=None):
    m, l, acc = carry
    m_new = jnp.maximum(m, jnp.max(s, axis=-1, keepdims=True))
    p = jnp.exp(s - m_new)
    if mask is not None:
        p = jnp.where(mask, p, 0.0)
    alpha = jnp.exp(m - m_new)
    l = alpha * l + jnp.sum(p, axis=-1, keepdims=True)
    acc = alpha * acc + _dot_nt(p.astype(BF16), vT)
    return m_new, l, acc


def _softmax_first(s, vT, mask):
    m = jnp.max(s, axis=-1, keepdims=True)
    p = jnp.where(mask, jnp.exp(s - m), 0.0)
    return m, jnp.sum(p, axis=-1, keepdims=True), _dot_nt(p.astype(BF16), vT)


def _fox_attn_kernel(q_ref, c_ref, kT_ref, vT_ref, cT_ref, o_ref):
    g = pl.program_id(1)
    iq = pl.program_id(2)
    tq = q_ref.shape[1]
    q2 = q_ref[0]
    cblk = c_ref[0]
    row = _iota((tq, tq), 0)
    col = _iota((tq, tq), 1)
    causal = col <= row
    outs = []
    for r in range(2):
        qh = q2[:, r * HEAD_DIM:(r + 1) * HEAD_DIM]
        cq = _lane_pick(cblk, 2 * g + r)

        def logits(j):
            off = pl.multiple_of(j * tq, tq)
            kT = kT_ref[0, :, pl.ds(off, tq)]
            vT = vT_ref[0, :, pl.ds(off, tq)]
            ck = cT_ref[0, 0, pl.ds(r, 1), pl.ds(off, tq)]
            return _dot(qh, kT) + (cq - ck), vT

        s, vT = logits(iq)
        carry = _softmax_first(jnp.where(causal, s, NEG_INF), vT, causal)

        def body(j, carry):
            s, vT = logits(j)
            return _softmax_step(carry, s, vT)

        m, l, acc = lax.fori_loop(0, iq, body, carry)
        outs.append(acc / jnp.maximum(l, TINY))
    o_ref[0] = jnp.concatenate(outs, axis=-1).astype(BF16)


def _fox_attn(q, c, fkvTb, cT):
    b, t, _ = q.shape
    tq = min(256, t)
    cT4 = cT.reshape(b, FOX_KV_HEADS, 2, t)
    return pl.pallas_call(
        _fox_attn_kernel, grid=(b, FOX_KV_HEADS, t // tq),
        in_specs=[pl.BlockSpec((1, tq, 128), lambda b, g, i: (b, i, g)),
                  pl.BlockSpec((1, tq, 128), lambda b, g, i: (b, i, 0)),
                  pl.BlockSpec((1, HEAD_DIM, t), lambda b, g, i: (b, g, 0)),
                  pl.BlockSpec((1, HEAD_DIM, t), lambda b, g, i: (b, FOX_KV_HEADS + g, 0)),
                  pl.BlockSpec((1, 1, 2, t), lambda b, g, i: (b, g, 0, 0))],
        out_specs=pl.BlockSpec((1, tq, 128), lambda b, g, i: (b, i, g)),
        out_shape=jax.ShapeDtypeStruct((b, t, 512), BF16),
        compiler_params=_cparams("parallel", "parallel", "arbitrary"), name="fox_attn")(q, c, fkvTb, fkvTb, cT4)


def _cmp_prompt_kernel(x_ref, pool_ref, peT_ref, wT_ref, kc_ref):
    x = x_ref[0]
    hi = x.astype(BF16)
    lo = (x - hi.astype(F32)).astype(BF16)
    pool = pool_ref[...]
    meanT = _dot(hi, pool) + _dot(lo, pool)
    meanT = meanT + jnp.mean(peT_ref[...], axis=-1, keepdims=True)
    kc_ref[0] = _dot(wT_ref[...], meanT.astype(BF16)).astype(BF16)


def _cmp_columns(n_blocks):
    j = np.arange(n_blocks)
    return np.where(j % 2 == 0, j // 2, n_blocks // 2 + j // 2)


def _cmp_prompt(cmpT, peT, wbdT):
    b, _, t = cmpT.shape
    nc = t // NSA_CMP_BLOCK
    cols = _cmp_columns(nc)[np.arange(t) // NSA_CMP_BLOCK]
    pool = jnp.asarray((cols[:, None] == np.arange(nc)[None, :]) / NSA_CMP_BLOCK, BF16)
    return pl.pallas_call(
        _cmp_prompt_kernel, grid=(b,),
        in_specs=[pl.BlockSpec((1, 256, t), lambda b: (b, 0, 0)),
                  pl.BlockSpec((t, nc), lambda b: (0, 0)),
                  pl.BlockSpec((256, NSA_CMP_BLOCK), lambda b: (0, 0)),
                  pl.BlockSpec((256, 256), lambda b: (0, 0))],
        out_specs=pl.BlockSpec((1, 256, nc), lambda b: (b, 0, 0)),
        out_shape=jax.ShapeDtypeStruct((b, 256, nc), BF16),
        compiler_params=_cparams("parallel"), name="nsa_compress")(cmpT, pool, peT, wbdT)


def _topk_select(val, blk_f, n_blocks, n_top):
    sel = jnp.zeros_like(val)
    for _ in range(n_top):
        mx = jnp.max(val, axis=-1, keepdims=True)
        idx = jnp.min(jnp.where(val == mx, blk_f, float(n_blocks)), axis=-1, keepdims=True)
        hit = blk_f == idx
        sel = jnp.where(hit, 1.0, sel)
        val = jnp.where(hit, -2.0, val)
    return sel


def _nsa_attn_kernel(q_ref, sm_ref, kcT_ref, vcT_ref, ksT_ref, vsT_ref, kwT_ref, vwT_ref, e_ref, o_ref, *, tk, nwin):
    g = pl.program_id(1)
    iq = pl.program_id(2)
    tq = q_ref.shape[1]
    nc = kcT_ref.shape[-1]
    ns = nc // 2
    s0 = iq * tq
    tpos = s0 + _iota((tq, 1), 0)
    q4 = q_ref[0]
    gates = sm_ref[0]
    qs = [q4[:, r * HEAD_DIM:(r + 1) * HEAD_DIM] for r in range(4)]
    slopes = [jnp.where(g == 0, NSA_SLOPES[r], NSA_SLOPES[4 + r]) for r in range(4)]

    lane = _iota((1, nc), 1)
    j_orig = jnp.where(lane < ns, 2 * lane, 2 * (lane - ns) + 1)
    cpos = NSA_CMP_BLOCK * (j_orig + 1) - 1
    dist_c = tpos - cpos
    mask_c = dist_c >= 0
    dist_cf = dist_c.astype(F32)
    kcT = kcT_ref[0]
    vcT = vcT_ref[0]
    imp = None
    o_cmp = []
    for r in range(4):
        s = jnp.where(mask_c, _dot(qs[r], kcT) - slopes[r] * dist_cf, NEG_INF)
        m = jnp.max(s, axis=-1, keepdims=True)
        p = jnp.where(mask_c, jnp.exp(s - m), 0.0)
        p = p / jnp.maximum(jnp.sum(p, axis=-1, keepdims=True), TINY)
        imp = p if imp is None else imp + p
        o_cmp.append(_dot_nt(p.astype(BF16), vcT))
    imp = imp[:, :ns] + imp[:, ns:]
    blk = _iota((1, ns), 1)
    cur = jnp.right_shift(tpos, 6)
    visible = blk * NSA_SEL_BLOCK <= tpos
    forced = (blk == 0) | (blk == cur) | (blk == cur - 1)
    val = jnp.where(visible, jnp.where(forced, FORCED_SCORE, imp), -1.0)
    sel = _topk_select(val, blk.astype(F32), ns, min(NSA_TOPK, ns))
    selb = jnp.where((sel > 0.0) & visible, 0.0, NEG_INF).astype(BF16)

    def sel_logits(j):
        off = pl.multiple_of(j * tk, tk)
        kT = ksT_ref[0, :, pl.ds(off, tk)]
        vT = vsT_ref[0, :, pl.ds(off, tk)]
        bias = _dot(selb, e_ref[:, pl.ds(off, tk)])
        dist = tpos - (off + _iota((1, tk), 1))
        return kT, vT, bias, dist

    n_full = s0 // tk
    kT, vT, bias, dist = sel_logits(n_full)
    ok = (dist >= 0) & (bias > 0.5 * NEG_INF)
    distf = dist.astype(F32)
    carries = []
    for r in range(4):
        s = jnp.where(ok, _dot(qs[r], kT) - slopes[r] * distf, NEG_INF)
        carries.append(_softmax_first(s, vT, ok))

    def body(j, carries):
        kT, vT, bias, dist = sel_logits(j)
        distf = dist.astype(F32)
        ok = bias > 0.5 * NEG_INF
        out = []
        for r in range(4):
            s = _dot(qs[r], kT) - slopes[r] * distf + bias
            out.append(_softmax_step(carries[r], s, vT, ok))
        return tuple(out)

    carries = lax.fori_loop(0, n_full, body, tuple(carries))
    o_sel = [acc / jnp.maximum(l, TINY) for (_, l, acc) in carries]

    start = pl.multiple_of(jnp.maximum(s0 - NSA_WINDOW, 0), 128)
    kT = kwT_ref[0, :, pl.ds(start, nwin)]
    vT = vwT_ref[0, :, pl.ds(start, nwin)]
    dist = tpos - (start + _iota((1, nwin), 1))
    ok = (dist >= 0) & (dist <= NSA_WINDOW)
    distf = dist.astype(F32)
    outs = []
    for r in range(4):
        s = jnp.where(ok, _dot(qs[r], kT) - slopes[r] * distf, NEG_INF)
        _, l, acc = _softmax_first(s, vT, ok)
        o_win = acc / jnp.maximum(l, TINY)
        base = FOX_HEADS + (g * 4 + r) * 3
        outs.append(_lane_pick(gates, base) * o_cmp[r] + _lane_pick(gates, base + 1) * o_sel[r]
                    + _lane_pick(gates, base + 2) * o_win)
    o_ref[0] = jnp.concatenate(outs, axis=-1).astype(BF16)


def _nsa_attn(q, sm, kcT, selTb, winTb):
    b, t, _ = q.shape
    tq = 128
    tk = min(512, t)
    nwin = min(NSA_WINDOW + tq, t)
    nc = kcT.shape[-1]
    ns = t // NSA_SEL_BLOCK
    e = jnp.asarray(np.arange(ns)[:, None] == (np.arange(t) // NSA_SEL_BLOCK)[None, :], BF16)
    tspec = lambda rows, width, off: pl.BlockSpec((1, rows, width), lambda b, g, i, off=off: (b, off + g, 0))
    return pl.pallas_call(
        functools.partial(_nsa_attn_kernel, tk=tk, nwin=nwin), grid=(b, NSA_KV_HEADS, t // tq),
        in_specs=[pl.BlockSpec((1, tq, 256), lambda b, g, i: (b, i, 2 + g)),
                  pl.BlockSpec((1, tq, 128), lambda b, g, i: (b, i, 0)),
                  tspec(HEAD_DIM, nc, 0), tspec(HEAD_DIM, nc, 2),
                  tspec(HEAD_DIM, t, 0), tspec(HEAD_DIM, t, 2),
                  tspec(HEAD_DIM, t, 0), tspec(HEAD_DIM, t, 2),
                  pl.BlockSpec((ns, t), lambda b, g, i: (0, 0))],
        out_specs=pl.BlockSpec((1, tq, 256), lambda b, g, i: (b, i, g)),
        out_shape=jax.ShapeDtypeStruct((b, t, 512), BF16),
        compiler_params=_cparams("parallel", "parallel", "arbitrary"), name="nsa_attn")(
            q, sm, kcT, kcT, selTb, selTb, winTb, winTb, e)


def _swa_attn_kernel(q_ref, kT_ref, vT_ref, sink_ref, o_ref, *, nkeys):
    g = pl.program_id(1)
    iq = pl.program_id(2)
    tq = q_ref.shape[1]
    s0 = iq * tq
    tpos = s0 + _iota((tq, 1), 0)
    start = pl.multiple_of(jnp.maximum(s0 - SWA_WINDOW, 0), 128)
    kT = kT_ref[0, :, pl.ds(start, nkeys)]
    vT = vT_ref[0, :, pl.ds(start, nkeys)]
    dist = tpos - (start + _iota((1, nkeys), 1))
    ok = (dist >= 0) & (dist <= SWA_WINDOW)
    distf = dist.astype(F32)
    q8 = q_ref[0]
    sinks = sink_ref[...]
    outs = []
    for r in range(8):
        slope = jnp.where(g == 0, SWA_SLOPES[r], SWA_SLOPES[8 + r])
        sink = _lane_pick(sinks, g * 8 + r)
        s = jnp.where(ok, _dot(q8[:, r * HEAD_DIM:(r + 1) * HEAD_DIM], kT) - slope * distf, NEG_INF)
        m = jnp.maximum(jnp.max(s, axis=-1, keepdims=True), sink)
        p = jnp.where(ok, jnp.exp(s - m), 0.0)
        den = jnp.sum(p, axis=-1, keepdims=True) + jnp.exp(sink - m)
        outs.append(_dot_nt(p.astype(BF16), vT) / jnp.maximum(den, TINY))
    o_ref[0] = jnp.concatenate(outs, axis=-1).astype(BF16)


def _swa_attn(q, kvTb, sinks):
    b, t, _ = q.shape
    tq = 128
    nkeys = min(SWA_WINDOW + tq, t)
    return pl.pallas_call(
        functools.partial(_swa_attn_kernel, nkeys=nkeys), grid=(b, SWA_KV_HEADS, t // tq),
        in_specs=[pl.BlockSpec((1, tq, 512), lambda b, g, i: (b, i, g)),
                  pl.BlockSpec((1, HEAD_DIM, t), lambda b, g, i: (b, g, 0)),
                  pl.BlockSpec((1, HEAD_DIM, t), lambda b, g, i: (b, 2 + g, 0)),
                  pl.BlockSpec((1, 128), lambda b, g, i: (0, 0))],
        out_specs=pl.BlockSpec((1, tq, 512), lambda b, g, i: (b, i, g)),
        out_shape=jax.ShapeDtypeStruct((b, t, 1024), BF16),
        compiler_params=_cparams("parallel", "parallel", "arbitrary"), name="swa_attn")(q, kvTb, kvTb, sinks)


def _out_ffn_kernel(*refs, n_mix, final, ff_chunk):
    x_ref = refs[0]
    o_refs = refs[1:1 + n_mix]
    wo_ref, gf_ref, wg_ref, wu_ref, wd_ref = refs[1 + n_mix:6 + n_mix]
    gfin_ref = refs[6 + n_mix] if final else None
    y_ref = refs[-1]
    mix = [o_ref[...] for o_ref in o_refs]
    mix = mix[0] if n_mix == 1 else jnp.concatenate(mix, axis=-1)
    x = x_ref[...] + _dot(mix, wo_ref[...])
    h = _rms(x, gf_ref[...]).astype(BF16)
    d_ff = wg_ref.shape[1]
    for c0 in range(0, d_ff, ff_chunk):
        gt = _dot(h, wg_ref[:, c0:c0 + ff_chunk])
        up = _dot(h, wu_ref[:, c0:c0 + ff_chunk])
        x = x + _dot((jax.nn.silu(gt) * up).astype(BF16), wd_ref[c0:c0 + ff_chunk, :])
    if final:
        x = _rms(x, gfin_ref[...])
    y_ref[...] = x


def _out_ffn(x, mixes, wo, gf, wg, wu, wd, gfin=None):
    rows, d = x.shape
    tm = min(512, rows)
    d_ff = wg.shape[1]
    ff_chunk = d_ff // 2 if d_ff % 256 == 0 else d_ff
    final = gfin is not None
    const = lambda i: (0, 0)
    resident = lambda shape: pl.BlockSpec(shape, const, pipeline_mode=pl.Buffered(1))
    in_specs = [pl.BlockSpec((tm, d), lambda i: (i, 0))]
    in_specs += [pl.BlockSpec((tm, m.shape[1]), lambda i: (i, 0)) for m in mixes]
    in_specs += [resident(wo.shape), pl.BlockSpec((1, d), const), resident(wg.shape), resident(wu.shape),
                 resident(wd.shape)]
    args = [x, *mixes, wo, gf, wg, wu, wd]
    if final:
        in_specs.append(pl.BlockSpec((1, d), const))
        args.append(gfin)
    return pl.pallas_call(
        functools.partial(_out_ffn_kernel, n_mix=len(mixes), final=final, ff_chunk=ff_chunk),
        grid=(rows // tm,), in_specs=in_specs,
        out_specs=pl.BlockSpec((tm, d), lambda i: (i, 0)),
        out_shape=jax.ShapeDtypeStruct((rows, d), F32),
        compiler_params=_cparams("parallel"), name="out_ffn")(*args)


def _fox_decode_kernel(pt_ref, qbd_ref, lfn_ref, knew_ref, vnew_ref, *refs, pps):
    kv_refs = refs[:pps]
    lf_refs = refs[pps:2 * pps]
    o_ref = refs[2 * pps]
    m_ref, l_ref, acc_ref, car_ref = refs[2 * pps + 1:]
    c = pl.program_id(1)
    qbd = qbd_ref[...]

    @pl.when(c == 0)
    def _():
        s_new = jnp.sum(qbd.astype(F32) * knew_ref[...], axis=-1, keepdims=True)
        m_ref[...] = jnp.broadcast_to(s_new, m_ref.shape)
        l_ref[...] = jnp.ones_like(l_ref)
        acc_ref[...] = jnp.broadcast_to(vnew_ref[...], acc_ref.shape)
        car_ref[...] = lfn_ref[...]

    m = m_ref[:, 0:1]
    l = l_ref[:, 0:1]
    acc = acc_ref[...]
    carry = car_ref[:, 0:1]
    lane = _iota((FOX_HEADS, PAGE_SIZE), 1)
    for i in range(pps):
        lf = lf_refs[i][...]
        z = jnp.where(lane < PAGE_SIZE - 1, pltpu.roll(lf, PAGE_SIZE - 1, axis=1), 0.0)
        k = 1
        while k < PAGE_SIZE:
            z = z + jnp.where(lane + k < PAGE_SIZE, pltpu.roll(z, PAGE_SIZE - k, axis=1), 0.0)
            k *= 2
        kT = kv_refs[i][0].astype(BF16)
        vT = kv_refs[i][1].astype(BF16)
        s = _dot(qbd, kT) + (z + carry)
        m, l, acc = _softmax_step((m, l, acc), s, vT)
        carry = carry + z[:, 0:1] + lf[:, 0:1]
    m_ref[...] = jnp.broadcast_to(m, m_ref.shape)
    l_ref[...] = jnp.broadcast_to(l, l_ref.shape)
    acc_ref[...] = acc
    car_ref[...] = jnp.broadcast_to(carry, car_ref.shape)

    @pl.when(c == pl.num_programs(1) - 1)
    def _():
        o_ref[...] = acc / jnp.maximum(l, TINY)


def _pages_per_step(n_pages, want):
    pps = want
    while n_pages % pps:
        pps //= 2
    return pps


def _fox_decode(page_table, qbd, lfn, knew, vnew, kv_pages, lf_pages):
    nb, npg = page_table.shape
    pps = _pages_per_step(npg, 16)

    def page_map(i):
        return lambda b, c, pt: (pt[b, npg - 1 - (c * pps + i)], 0, 0, 0)

    def lf_map(i):
        return lambda b, c, pt: (pt[b, npg - 1 - (c * pps + i)], 0, 0)

    per_b3 = lambda b, c, pt: (b, 0, 0)
    in_specs = [pl.BlockSpec((None, FOX_HEADS, 256), per_b3),
                pl.BlockSpec((None, FOX_HEADS, 128), per_b3),
                pl.BlockSpec((None, 1, 256), per_b3),
                pl.BlockSpec((None, 1, 256), per_b3)]
    in_specs += [pl.BlockSpec((None, 2, 256, PAGE_SIZE), page_map(i)) for i in range(pps)]
    in_specs += [pl.BlockSpec((None, FOX_HEADS, PAGE_SIZE), lf_map(i)) for i in range(pps)]
    grid_spec = pltpu.PrefetchScalarGridSpec(
        num_scalar_prefetch=1, grid=(nb, npg // pps), in_specs=in_specs,
        out_specs=pl.BlockSpec((None, FOX_HEADS, 256), per_b3),
        scratch_shapes=[pltpu.VMEM((FOX_HEADS, 128), F32), pltpu.VMEM((FOX_HEADS, 128), F32),
                        pltpu.VMEM((FOX_HEADS, 256), F32), pltpu.VMEM((FOX_HEADS, 128), F32)])
    return pl.pallas_call(
        functools.partial(_fox_decode_kernel, pps=pps), grid_spec=grid_spec,
        out_shape=jax.ShapeDtypeStruct((nb, FOX_HEADS, 256), F32),
        compiler_params=_cparams("parallel", "arbitrary"), name="fox_decode")(
            page_table, qbd, lfn, knew, vnew, *([kv_pages] * pps), *([lf_pages] * pps))


def _cmp_decode_kernel(pt_ref, pool_ref, pe_ref, w_ref, *refs, pps):
    x_refs = refs[:pps]
    kc_ref = refs[pps]
    mean = None
    for i in range(pps):
        x = x_refs[i][...]
        hi = x.astype(BF16)
        lo = (x - hi.astype(F32)).astype(BF16)
        part = _dot_nt(pool_ref[i], hi) + _dot_nt(pool_ref[i], lo)
        mean = part if mean is None else mean + part
    mean = mean + jnp.mean(pe_ref[...], axis=0, keepdims=True)
    summ = _dot(mean.astype(BF16), w_ref[...])
    half = 2 * pps
    kc_ref[0] = summ[:half]
    kc_ref[1] = summ[half:]


def _cmp_decode(page_table, cmp_pages, pe_rows, wbd):
    nb, npg = page_table.shape
    pps = _pages_per_step(npg, 16)
    per_page = PAGE_SIZE // NSA_CMP_BLOCK
    lane_blk = np.arange(PAGE_SIZE) // NSA_CMP_BLOCK
    pool = np.zeros((pps, per_page * pps, PAGE_SIZE), np.float32)
    for i in range(pps):
        row = (lane_blk % 2) * (2 * pps) + 2 * i + lane_blk // 2
        pool[i, row, np.arange(PAGE_SIZE)] = 1.0 / NSA_CMP_BLOCK
    pool = jnp.asarray(pool, BF16)

    def page_map(i):
        return lambda b, c, pt: (pt[b, c * pps + i], 0, 0)

    in_specs = [pl.BlockSpec(pool.shape, lambda b, c, pt: (0, 0, 0)),
                pl.BlockSpec(pe_rows.shape, lambda b, c, pt: (0, 0)),
                pl.BlockSpec((256, 256), lambda b, c, pt: (0, 0))]
    in_specs += [pl.BlockSpec((None, 256, PAGE_SIZE), page_map(i)) for i in range(pps)]
    n_half = 2 * npg
    grid_spec = pltpu.PrefetchScalarGridSpec(
        num_scalar_prefetch=1, grid=(nb, npg // pps), in_specs=in_specs,
        out_specs=pl.BlockSpec((None, 2, 2 * pps, 256), lambda b, c, pt: (b, 0, c, 0)))
    return pl.pallas_call(
        functools.partial(_cmp_decode_kernel, pps=pps), grid_spec=grid_spec,
        out_shape=jax.ShapeDtypeStruct((nb, 2, n_half, 256), F32),
        compiler_params=_cparams("parallel", "arbitrary"), name="nsa_cmp_decode")(
            page_table, pool, pe_rows, wbd, *([cmp_pages] * pps))


def _nsa_select_kernel(qbd_ref, kc_ref, slope_ref, ocmp_ref, idx_ref, *, past, n_pick):
    n_half = kc_ref.shape[1]
    nc = 2 * n_half
    kc = kc_ref[...].reshape(nc, 256)
    lane = _iota((1, nc), 1)
    j_orig = jnp.where(lane < n_half, 2 * lane, 2 * (lane - n_half) + 1)
    dist = past - (NSA_CMP_BLOCK * (j_orig + 1) - 1)
    ok = dist >= 0
    s = _dot_nt(qbd_ref[...], kc[:, :128].astype(BF16)) - slope_ref[:, 0:1] * dist.astype(F32)
    s = jnp.where(ok, s, NEG_INF)
    m = jnp.max(s, axis=-1, keepdims=True)
    p = jnp.where(ok, jnp.exp(s - m), 0.0)
    p = p / jnp.maximum(jnp.sum(p, axis=-1, keepdims=True), TINY)
    ocmp_ref[...] = _dot(p.astype(BF16), kc[:, 128:].astype(BF16))
    blk = _iota((1, n_half), 1)
    blk_f = blk.astype(F32)
    slot = _iota((1, NSA_TOPK), 1)
    forced = (blk == 0) | (blk == n_half - 1)
    for g in range(NSA_KV_HEADS):
        imp = p[4 * g:4 * g + 1] + p[4 * g + 1:4 * g + 2] + p[4 * g + 2:4 * g + 3] + p[4 * g + 3:4 * g + 4]
        val = jnp.where(forced, FORCED_SCORE, imp[:, :n_half] + imp[:, n_half:])
        picks = jnp.full((1, NSA_TOPK), n_half, jnp.int32)
        for it in range(n_pick):
            mx = jnp.max(val, axis=-1, keepdims=True)
            idx = jnp.min(jnp.where(val == mx, blk_f, float(n_half)), axis=-1, keepdims=True)
            val = jnp.where(blk_f == idx, -2.0, val)
            picks = jnp.where(slot == it, idx.astype(jnp.int32), picks)
        idx_ref[pl.ds(g, 1), :] = picks


def _nsa_select(qbd, kc, slopes, past):
    nb, _, n_half, _ = kc.shape
    n_pick = min(NSA_TOPK, n_half + 1) - 1
    return pl.pallas_call(
        functools.partial(_nsa_select_kernel, past=past, n_pick=n_pick), grid=(nb,),
        in_specs=[pl.BlockSpec((None, NSA_HEADS, 128), lambda b: (b, 0, 0)),
                  pl.BlockSpec((None, 2, n_half, 256), lambda b: (b, 0, 0, 0)),
                  pl.BlockSpec((NSA_HEADS, 128), lambda b: (0, 0))],
        out_specs=[pl.BlockSpec((None, NSA_HEADS, 128), lambda b: (b, 0, 0)),
                   pl.BlockSpec((None, NSA_KV_HEADS, NSA_TOPK), lambda b: (b, 0, 0))],
        out_shape=[jax.ShapeDtypeStruct((nb, NSA_HEADS, 128), F32),
                   jax.ShapeDtypeStruct((nb, NSA_KV_HEADS, NSA_TOPK), jnp.int32)],
        compiler_params=_cparams("parallel"), name="nsa_select")(qbd, kc, slopes)


def _shift_in(state, new_col):
    n = state.shape[-1]
    lane = _iota(state.shape, 1)
    return jnp.where(lane < n - 1, pltpu.roll(state, n - 1, axis=1), new_col)


def _nsa_decode_kernel(pt_ref, idx_ref, q_ref, new_ref, gate_ref, ocmp_ref, slope_ref, win_ref, *refs,
                       past, n_pick):
    pages = refs[:n_pick]
    o_ref, wout_ref = refs[n_pick:]
    b = pl.program_id(0)
    g = pl.program_id(1)
    q = q_ref[...]
    qf = q.astype(F32)
    new = new_ref[...]
    newb = new.astype(BF16).astype(F32)
    slope = slope_ref[:, 0:1]
    lane = _iota((1, PAGE_SIZE), 1)

    m = jnp.sum(qf * newb[0:1], axis=-1, keepdims=True)
    l = jnp.ones_like(m)
    acc = jnp.broadcast_to(newb[1:2], (8, HEAD_DIM))
    for i in range(n_pick):
        blk = idx_ref[b * NSA_KV_HEADS + g, i]
        half = blk % 2
        ok = jnp.right_shift(lane, 6) == half
        pos = blk * NSA_SEL_BLOCK + lane - NSA_SEL_BLOCK * half
        s = _dot(q, pages[i][0].astype(BF16)) - slope * (past - pos).astype(F32)
        m, l, acc = _softmax_step((m, l, acc), jnp.where(ok, s, NEG_INF), pages[i][1].astype(BF16), ok)
    o_sel = acc / jnp.maximum(l, TINY)

    n_buf = win_ref.shape[-1]
    kT = win_ref[0]
    vT = win_ref[1]
    dist = (n_buf - _iota((1, n_buf), 1)).astype(F32)
    m = jnp.sum(qf * newb[2:3], axis=-1, keepdims=True)
    carry = (m, jnp.ones_like(m), jnp.broadcast_to(newb[3:4], (8, HEAD_DIM)))
    _, l, acc = _softmax_step(carry, _dot(q, kT.astype(BF16)) - slope * dist, vT.astype(BF16))
    o_win = acc / jnp.maximum(l, TINY)

    gate = gate_ref[...]
    o_ref[...] = gate[:, 0:1] * ocmp_ref[...] + gate[:, 1:2] * o_sel + gate[:, 2:3] * o_win
    wout_ref[0] = _shift_in(kT, _row_to_col(new[2:3]))
    wout_ref[1] = _shift_in(vT, _row_to_col(new[3:4]))


def _nsa_decode(page_table, idx, q, new, gate, ocmp, slopes, win_state, sel_pages, past):
    nb, npg = page_table.shape
    n_buf = win_state.shape[-1]
    n_pick = min(NSA_TOPK, 2 * npg + 1) - 1
    idx2 = idx.reshape(nb * NSA_KV_HEADS, NSA_TOPK)

    def page_map(i):
        return lambda b, g, pt, ix: (pt[b, ix[b * NSA_KV_HEADS + g, i] // 2], 0, g, 0, 0)

    per_bg = lambda b, g, pt, ix: (b, g, 0, 0)
    in_specs = [pl.BlockSpec((None, None, 8, HEAD_DIM), per_bg),
                pl.BlockSpec((None, None, 4, HEAD_DIM), per_bg),
                pl.BlockSpec((None, None, 8, 128), per_bg),
                pl.BlockSpec((None, None, 8, HEAD_DIM), per_bg),
                pl.BlockSpec((None, 8, 128), lambda b, g, pt, ix: (g, 0, 0)),
                pl.BlockSpec((None, 2, None, HEAD_DIM, n_buf), lambda b, g, pt, ix: (b, 0, g, 0, 0))]
    in_specs += [pl.BlockSpec((None, 2, None, HEAD_DIM, PAGE_SIZE), page_map(i)) for i in range(n_pick)]
    grid_spec = pltpu.PrefetchScalarGridSpec(
        num_scalar_prefetch=2, grid=(nb, NSA_KV_HEADS), in_specs=in_specs,
        out_specs=[pl.BlockSpec((None, None, 8, HEAD_DIM), per_bg),
                   pl.BlockSpec((None, 2, None, HEAD_DIM, n_buf), lambda b, g, pt, ix: (b, 0, g, 0, 0))])
    return pl.pallas_call(
        functools.partial(_nsa_decode_kernel, past=past, n_pick=n_pick), grid_spec=grid_spec,
        out_shape=[jax.ShapeDtypeStruct((nb, NSA_KV_HEADS, 8, HEAD_DIM), F32),
                   jax.ShapeDtypeStruct(win_state.shape, F32)],
        compiler_params=_cparams("parallel", "arbitrary"), name="nsa_decode")(
            page_table, idx2, q, new, gate, ocmp, slopes, win_state, *([sel_pages] * n_pick))


def _swa_decode_kernel(qbd_ref, new_ref, sink_ref, slope_ref, st_ref, o_ref, sout_ref):
    n_buf = st_ref.shape[-1]
    qbd = qbd_ref[...]
    new = new_ref[...]
    newb = new.astype(BF16).astype(F32)
    kT = st_ref[0].reshape(128, n_buf)
    vT = st_ref[1].reshape(128, n_buf)
    dist = (n_buf - _iota((1, n_buf), 1)).astype(F32)
    s = _dot(qbd, kT.astype(BF16)) - slope_ref[:, 0:1] * dist
    s_new = jnp.sum(qbd.astype(F32) * newb[:, :128], axis=-1, keepdims=True)
    sink = sink_ref[:, 0:1]
    m = jnp.maximum(jnp.maximum(jnp.max(s, axis=-1, keepdims=True), s_new), sink)
    p = jnp.exp(s - m)
    p_new = jnp.exp(s_new - m)
    den = jnp.sum(p, axis=-1, keepdims=True) + p_new + jnp.exp(sink - m)
    acc = _dot_nt(p.astype(BF16), vT.astype(BF16)) + p_new.astype(BF16).astype(F32) * newb[:, 128:]
    o_ref[...] = acc / jnp.maximum(den, TINY)
    sout_ref[0] = _shift_in(kT, _row_to_col(new[:, :128])).reshape(st_ref.shape[1:])
    sout_ref[1] = _shift_in(vT, _row_to_col(new[:, 128:])).reshape(st_ref.shape[1:])


def _swa_decode(qbd, new, sinks, slopes, state):
    nb = qbd.shape[0]
    n_buf = state.shape[-1]
    st_spec = pl.BlockSpec((None, 2, SWA_KV_HEADS, HEAD_DIM, n_buf), lambda b: (b, 0, 0, 0, 0))
    return pl.pallas_call(
        _swa_decode_kernel, grid=(nb,),
        in_specs=[pl.BlockSpec((None, SWA_HEADS, 128), lambda b: (b, 0, 0)),
                  pl.BlockSpec((None, 1, 256), lambda b: (b, 0, 0)),
                  pl.BlockSpec((SWA_HEADS, 128), lambda b: (0, 0)),
                  pl.BlockSpec((SWA_HEADS, 128), lambda b: (0, 0)),
                  st_spec],
        out_specs=[pl.BlockSpec((None, SWA_HEADS, 128), lambda b: (b, 0, 0)), st_spec],
        out_shape=[jax.ShapeDtypeStruct((nb, SWA_HEADS, 128), F32),
                   jax.ShapeDtypeStruct(state.shape, F32)],
        compiler_params=_cparams("parallel"), name="swa_decode")(qbd, new, sinks, slopes, state)


def _block_diag_q(q, n_groups, per_group):
    nb = q.shape[0]
    q5 = q.reshape(nb, n_groups, per_group, 1, HEAD_DIM)
    eye = jnp.eye(n_groups, dtype=q.dtype).reshape(1, n_groups, 1, n_groups, 1)
    return (q5 * eye).reshape(nb, n_groups * per_group, n_groups * HEAD_DIM)


def _block_diag_take(o, n_groups, per_group):
    nb = o.shape[0]
    o5 = o.reshape(nb, n_groups, per_group, n_groups, HEAD_DIM)
    picked = jnp.stack([o5[:, g, :, g, :] for g in range(n_groups)], axis=1)
    return picked.reshape(nb, n_groups * per_group * HEAD_DIM)


def _t_last_view(x):
    nd = x.ndim
    return jnp.transpose(x, (0,) + tuple(range(2, nd)) + (1,))


def _t_first_view(x):
    nd = x.ndim
    return jnp.transpose(x, (0, nd - 1) + tuple(range(1, nd - 1)))


def kernel(x_prompt, x_sample, cache_fox_kv, cache_fox_logf, cache_nsa_cmp_kv, cache_nsa_sel_kv, state_nsa_win,
           state_swa, page_table, even_w_in, even_b_forget, even_pe_cmp, even_w_cmp, even_w_out, odd_w_in,
           odd_sinks, odd_w_out, norm_mix, norm_ffn, ffn_w_gate, ffn_w_up, ffn_w_down, norm_final):
    b, t, d = x_prompt.shape
    nb = x_sample.shape[0]
    npg = page_table.shape[1]
    past = npg * PAGE_SIZE
    assert norm_mix.shape[0] == 2 and x_sample.shape[1] == 1
    assert past >= NSA_WINDOW and t % 512 == 0

    w = even_w_in[0]
    fq, fkv, fl = w[:, 0:512], w[:, 512:1024], w[:, 1024:1032]
    nq, nkv, ng = w[:, 1032:1544], w[:, 1544:2312], w[:, 2312:2336]
    e_wn = jnp.concatenate([fq * ATTN_SCALE, nq * ATTN_SCALE, fl, ng, jnp.zeros((d, 96), F32)], axis=1).astype(BF16)
    e_wt = jnp.concatenate([fkv, nkv, fl], axis=1).T.astype(BF16)
    e_wr = jnp.concatenate([fkv, nkv], axis=1).astype(BF16)
    bsm = jnp.zeros((1, 128), F32).at[0, :FOX_HEADS].set(even_b_forget[0])
    bcol = even_b_forget[0].reshape(FOX_HEADS, 1)
    w_cmp = even_w_cmp[0].reshape(4, HEAD_DIM, HEAD_DIM)
    wbd = jnp.zeros((256, 256), F32)
    for i in range(4):
        wbd = wbd.at[i * 64:(i + 1) * 64, i * 64:(i + 1) * 64].set(w_cmp[i])
    wbd_b = wbd.astype(BF16)
    wbdT_b = wbd.T.astype(BF16)
    pe_rows = even_pe_cmp[0].reshape(NSA_CMP_BLOCK, 256)
    o_w = odd_w_in[0]
    o_wn = (o_w[:, :1024] * ATTN_SCALE).astype(BF16)
    o_wt = o_w[:, 1024:].T.astype(BF16)
    o_wr = o_w[:, 1024:].astype(BF16)
    e_wo = even_w_out[0].astype(BF16)
    o_wo = odd_w_out[0].astype(BF16)
    wg = ffn_w_gate.astype(BF16)
    wu = ffn_w_up.astype(BF16)
    wd = ffn_w_down.astype(BF16)
    g_mix = norm_mix.reshape(2, 1, d)
    g_ffn = norm_ffn.reshape(2, 1, d)
    g_fin = norm_final.reshape(1, d)
    sinks_row = jnp.zeros((1, 128), F32).at[0, :SWA_HEADS].set(odd_sinks[0])
    sinks_col = jnp.broadcast_to(odd_sinks[0].reshape(SWA_HEADS, 1), (SWA_HEADS, 128))
    nsa_slopes8 = jnp.broadcast_to(jnp.asarray(NSA_SLOPES, F32).reshape(NSA_HEADS, 1), (NSA_HEADS, 128))
    nsa_slopes_g = jnp.concatenate([nsa_slopes8.reshape(2, 4, 128), jnp.zeros((2, 4, 128), F32)], axis=1)
    swa_slopes16 = jnp.broadcast_to(jnp.asarray(SWA_SLOPES, F32).reshape(SWA_HEADS, 1), (SWA_HEADS, 128))

    fox_pages = _t_last_view(cache_fox_kv[0]).reshape(-1, 2, 256, PAGE_SIZE)
    lf_pages = _t_last_view(cache_fox_logf[0])
    cmp_pages = _t_last_view(cache_nsa_cmp_kv[0]).reshape(-1, 256, PAGE_SIZE)
    sel_pages = _t_last_view(cache_nsa_sel_kv[0])
    win_state = _t_last_view(state_nsa_win[0])
    swa_state = _t_last_view(state_swa[0])

    (q_p, sm_p, fkvT, cmpT, selT, winT, logfT, fkvTb, selTb, winTb) = _even_proj(
        x_prompt, g_mix[0], e_wn, e_wt, bsm, bcol)
    cT, c = _cumsum(logfT, sm_p)
    o_fox = _fox_attn(q_p, c, fkvTb, cT)
    kcT = _cmp_prompt(cmpT, pe_rows.T, wbdT_b)
    o_nsa = _nsa_attn(q_p, sm_p, kcT, selTb, winTb)
    xp = _out_ffn(x_prompt.reshape(b * t, d), [o_fox.reshape(b * t, 512), o_nsa.reshape(b * t, 512)],
                  e_wo, g_ffn[0], wg[0], wu[0], wd[0])

    xs0 = x_sample.reshape(1, nb, d)
    (q_s, sm_s, fkvT_s, cmpT_s, selT_s, winT_s, logfT_s, _, _, _, rows_s) = _even_proj(
        xs0, g_mix[0], e_wn, e_wt, bsm, bcol, e_wr)
    q_s, sm_s, rows_s = q_s[0], sm_s[0], rows_s[0]
    rows_sb = rows_s.astype(BF16).astype(F32)
    qbd_f = _block_diag_q(q_s[:, :512], FOX_KV_HEADS, 2)
    lfn = jnp.broadcast_to(sm_s[:, :FOX_HEADS, None], (nb, FOX_HEADS, 128))
    o_fox_s = _fox_decode(page_table, qbd_f, lfn, rows_sb[:, None, 0:256], rows_sb[:, None, 256:512],
                          fox_pages, lf_pages)
    o_fox_s = _block_diag_take(o_fox_s, FOX_KV_HEADS, 2)
    kc_s = _cmp_decode(page_table, cmp_pages, pe_rows, wbd_b)
    qbd_n = _block_diag_q(q_s[:, 512:], NSA_KV_HEADS, 4)
    ocmp_s, idx_s = _nsa_select(qbd_n, kc_s, nsa_slopes8, past)
    ocmp_s = _block_diag_take(ocmp_s, NSA_KV_HEADS, 4).reshape(nb, NSA_KV_HEADS, 4, HEAD_DIM)
    pad4 = jnp.zeros((nb, NSA_KV_HEADS, 4, HEAD_DIM), F32)
    ocmp_s = jnp.concatenate([ocmp_s, pad4], axis=2)
    qn_s = jnp.concatenate([q_s[:, 512:].reshape(nb, NSA_KV_HEADS, 4, HEAD_DIM), pad4.astype(BF16)], axis=2)
    nkv_s = rows_s[:, 512:].reshape(nb, 6, NSA_KV_HEADS, HEAD_DIM)
    new_s = jnp.transpose(nkv_s[:, 2:6], (0, 2, 1, 3))
    gate_s = sm_s[:, FOX_HEADS:FOX_HEADS + 24].reshape(nb, NSA_KV_HEADS, 4, 3)
    gate_s = jnp.zeros((nb, NSA_KV_HEADS, 8, 128), F32).at[:, :, :4, :3].set(gate_s)
    o_nsa_s, win_new = _nsa_decode(page_table, idx_s, qn_s, new_s, gate_s, ocmp_s, nsa_slopes_g, win_state,
                                   sel_pages, past)
    o_nsa_s = o_nsa_s[:, :, :4].reshape(nb, 512)
    xs = _out_ffn(x_sample.reshape(nb, d), [o_fox_s.astype(BF16), o_nsa_s.astype(BF16)],
                  e_wo, g_ffn[0], wg[0], wu[0], wd[0])

    q1, kvT1, kvT1b = _odd_proj(xp.reshape(b, t, d), g_mix[1], o_wn, o_wt)
    o_swa = _swa_attn(q1, kvT1b, sinks_row)
    y_prompt = _out_ffn(xp, [o_swa.reshape(b * t, 1024)], o_wo, g_ffn[1], wg[1], wu[1], wd[1], g_fin)

    q1s, kvT1s, _, rows1s = _odd_proj(xs.reshape(1, nb, d), g_mix[1], o_wn, o_wt, o_wr)
    qbd_s = _block_diag_q(q1s[0], SWA_KV_HEADS, 8)
    o_swa_s, swa_new = _swa_decode(qbd_s, rows1s[0][:, None, :], sinks_col, swa_slopes16, swa_state)
    o_swa_s = _block_diag_take(o_swa_s, SWA_KV_HEADS, 8)
    y_sample = _out_ffn(xs, [o_swa_s.astype(BF16)], o_wo, g_ffn[1], wg[1], wu[1], wd[1], g_fin)

    def rows_first(xT, *feat):
        n, _, rows = xT.shape
        return _t_first_view(xT.reshape((n,) + feat + (rows,)))[None]

    n_win = min(NSA_WINDOW, t)
    n_swa = min(SWA_WINDOW, t)
    return (y_prompt.reshape(b, t, d), y_sample.reshape(nb, 1, d),
            rows_first(fkvT, 2, FOX_KV_HEADS, HEAD_DIM),
            jnp.transpose(rows_first(fkvT_s, 2, FOX_KV_HEADS, HEAD_DIM), (0, 2, 1, 3, 4, 5)),
            rows_first(logfT, FOX_HEADS),
            jnp.transpose(rows_first(logfT_s, FOX_HEADS), (0, 2, 1, 3)),
            rows_first(cmpT, 2, NSA_KV_HEADS, HEAD_DIM),
            jnp.transpose(rows_first(cmpT_s, 2, NSA_KV_HEADS, HEAD_DIM), (0, 2, 1, 3, 4, 5)),
            rows_first(selT, 2, NSA_KV_HEADS, HEAD_DIM),
            jnp.transpose(rows_first(selT_s, 2, NSA_KV_HEADS, HEAD_DIM), (0, 2, 1, 3, 4, 5)),
            rows_first(winT[:, :, t - n_win:], 2, NSA_KV_HEADS, HEAD_DIM),
            _t_first_view(win_new)[None],
            rows_first(kvT1[:, :, t - n_swa:], 2, SWA_KV_HEADS, HEAD_DIM),
            _t_first_view(swa_new)[None])
```

```python
import functools

import numpy as np
import jax
import jax.numpy as jnp
from jax import lax
from jax.experimental import pallas as pl
from jax.experimental.pallas import tpu as pltpu

F32 = jnp.float32
BF16 = jnp.bfloat16

HEAD_DIM = 64
PAGE_SIZE = 128
FOX_HEADS = 8
FOX_KV_HEADS = 4
NSA_HEADS = 8
NSA_KV_HEADS = 2
NSA_CMP_BLOCK = 32
NSA_SEL_BLOCK = 64
NSA_TOPK = 16
NSA_WINDOW = 512
SWA_HEADS = 16
SWA_KV_HEADS = 2
SWA_WINDOW = 128
ATTN_SCALE = HEAD_DIM ** -0.5
RMS_EPS = 1e-6
NEG_INF = -1e30
TINY = 1e-30
FORCED_SCORE = 1e6
FOX_TQ = 1024
NSA_TQ = 256

VMEM_LIMIT_BYTES = 56 * 1024 * 1024


def _alibi_slopes(n_heads):
    exps = np.arange(1, n_heads + 1, dtype=np.float32)
    return np.power(np.float32(2.0), -8.0 * exps / n_heads).astype(np.float32)


NSA_SLOPES = [float(v) for v in _alibi_slopes(NSA_HEADS)]
SWA_SLOPES = [float(v) for v in _alibi_slopes(SWA_HEADS)]


def _cparams(*sem):
    return pltpu.CompilerParams(dimension_semantics=sem, vmem_limit_bytes=VMEM_LIMIT_BYTES)


def _dot(a, b):
    return jnp.dot(a, b, preferred_element_type=F32)


def _dot_nt(a, b):
    return lax.dot_general(a, b, (((1,), (1,)), ((), ())), preferred_element_type=F32)


def _split3(x):
    hi = x.astype(BF16)
    r1 = x - hi.astype(F32)
    mid = r1.astype(BF16)
    lo = (r1 - mid.astype(F32)).astype(BF16)
    return hi, mid, lo


def _rms(x, g):
    return x * lax.rsqrt(jnp.mean(x * x, axis=-1, keepdims=True) + RMS_EPS) * g


def _log_sigmoid(x):
    return jnp.minimum(x, 0.0) - jnp.log1p(jnp.exp(-jnp.abs(x)))


def _iota(shape, dim):
    return lax.broadcasted_iota(jnp.int32, shape, dim)


def _lane_pick(x, lane_idx):
    lane = _iota(x.shape, 1)
    return jnp.sum(jnp.where(lane == lane_idx, x, 0.0), axis=-1, keepdims=True)


def _row_to_col(row):
    n = row.shape[-1]
    full = jnp.broadcast_to(row, (n, n))
    return jnp.sum(jnp.where(_iota((n, n), 0) == _iota((n, n), 1), full, 0.0), axis=-1, keepdims=True)


def _pos_features(pos):
    row = _iota((8, pos.shape[-1]), 0)
    hi = jnp.left_shift(jnp.right_shift(pos, 7), 7).astype(F32)
    lo = jnp.bitwise_and(pos, 127).astype(F32)
    return jnp.where(row < 2, 1.0, jnp.where(row == 2, hi, jnp.where(row == 3, lo, 0.0)))


def _alibi_query(qh, tpos, slope):
    lane = _iota(qh.shape, 1)
    hi = jnp.left_shift(jnp.right_shift(tpos, 7), 7).astype(F32)
    lo = jnp.bitwise_and(tpos, 127).astype(F32)
    a_hi = (-slope * hi).astype(BF16)
    a_lo = (-slope * lo).astype(BF16)
    m = (jnp.zeros_like(hi) + slope).astype(BF16)
    return jnp.where(lane == 64, a_hi, jnp.where(lane == 65, a_lo, jnp.where((lane == 66) | (lane == 67), m, qh)))


def _even_proj_kernel(*refs, with_rows):
    if with_rows:
        (x_ref, g_ref, wn_ref, wt_ref, bsm_ref, bcol_ref, wr_ref,
         q_ref, sm_ref, fkvT_ref, cmpT_ref, selT_ref, winT_ref, logfT_ref,
         fkvTb_ref, selTb_ref, winTb_ref, kselA_ref, kwinA_ref, rows_ref) = refs
    else:
        (x_ref, g_ref, wn_ref, wt_ref, bsm_ref, bcol_ref,
         q_ref, sm_ref, fkvT_ref, cmpT_ref, selT_ref, winT_ref, logfT_ref,
         fkvTb_ref, selTb_ref, winTb_ref, kselA_ref, kwinA_ref) = refs
    tm = x_ref.shape[1]
    h = _rms(x_ref[0], g_ref[...]).astype(BF16)
    n = _dot(h, wn_ref[...])
    q_ref[0] = n[:, :2048].astype(BF16)
    sm = n[:, 2048:] + bsm_ref[...]
    lane = _iota(sm.shape, 1)
    sm_ref[0] = jnp.where(lane < FOX_HEADS, _log_sigmoid(sm), jax.nn.sigmoid(sm))
    t = _dot_nt(wt_ref[...], h)
    fkv = t[0:512]
    sel = t[768:1024]
    win = t[1024:1280]
    fkvT_ref[0] = fkv
    cmpT_ref[0] = t[512:768]
    selT_ref[0] = sel
    winT_ref[0] = win
    logfT_ref[0] = _log_sigmoid(t[1280:1288] + bcol_ref[...])
    fkvTb_ref[0] = fkv.astype(BF16)
    selTb_ref[0] = sel.astype(BF16)
    winTb_ref[0] = win.astype(BF16)
    ns = kselA_ref.shape[2] - 128
    pos = pl.program_id(1) * tm + _iota((1, tm), 1)
    feat = _pos_features(pos)
    pad = jnp.zeros((56, tm), F32)
    onehot = (_iota((ns, tm), 0) == jnp.right_shift(pos, 6)).astype(F32)
    for g in range(NSA_KV_HEADS):
        ksel = sel[g * HEAD_DIM:(g + 1) * HEAD_DIM]
        kwin = win[g * HEAD_DIM:(g + 1) * HEAD_DIM]
        kselA_ref[0, g] = jnp.concatenate([ksel, feat, pad, onehot], axis=0).astype(BF16)
        kwinA_ref[0, g] = jnp.concatenate([kwin, feat, pad], axis=0).astype(BF16)
    if with_rows:
        rows_ref[0] = _dot(h, wr_ref[...])


def _even_proj(x, g, wn, wt, bsm, bcol, wr=None):
    bx, tx, d = x.shape
    tm = min(512, tx)
    ns = max(tx // NSA_SEL_BLOCK, 8)
    with_rows = wr is not None
    const2 = lambda b, j: (0, 0)
    in_specs = [pl.BlockSpec((1, tm, d), lambda b, j: (b, j, 0)),
                pl.BlockSpec((1, d), const2),
                pl.BlockSpec(wn.shape, const2),
                pl.BlockSpec(wt.shape, const2),
                pl.BlockSpec((1, 128), const2),
                pl.BlockSpec((FOX_HEADS, 1), const2)]
    args = [x, g, wn, wt, bsm, bcol]
    tspec = lambda rows: pl.BlockSpec((1, rows, tm), lambda b, j: (b, 0, j))
    aspec = lambda rows: pl.BlockSpec((1, NSA_KV_HEADS, rows, tm), lambda b, j: (b, 0, 0, j))
    out_specs = [pl.BlockSpec((1, tm, 2048), lambda b, j: (b, j, 0)),
                 pl.BlockSpec((1, tm, 128), lambda b, j: (b, j, 0)),
                 tspec(512), tspec(256), tspec(256), tspec(256), tspec(FOX_HEADS),
                 tspec(512), tspec(256), tspec(256), aspec(128 + ns), aspec(128)]
    out_shape = [jax.ShapeDtypeStruct((bx, tx, 2048), BF16),
                 jax.ShapeDtypeStruct((bx, tx, 128), F32),
                 jax.ShapeDtypeStruct((bx, 512, tx), F32),
                 jax.ShapeDtypeStruct((bx, 256, tx), F32),
                 jax.ShapeDtypeStruct((bx, 256, tx), F32),
                 jax.ShapeDtypeStruct((bx, 256, tx), F32),
                 jax.ShapeDtypeStruct((bx, FOX_HEADS, tx), F32),
                 jax.ShapeDtypeStruct((bx, 512, tx), BF16),
                 jax.ShapeDtypeStruct((bx, 256, tx), BF16),
                 jax.ShapeDtypeStruct((bx, 256, tx), BF16),
                 jax.ShapeDtypeStruct((bx, NSA_KV_HEADS, 128 + ns, tx), BF16),
                 jax.ShapeDtypeStruct((bx, NSA_KV_HEADS, 128, tx), BF16)]
    if with_rows:
        in_specs.append(pl.BlockSpec(wr.shape, const2))
        args.append(wr)
        out_specs.append(pl.BlockSpec((1, tm, wr.shape[1]), lambda b, j: (b, j, 0)))
        out_shape.append(jax.ShapeDtypeStruct((bx, tx, wr.shape[1]), F32))
    return pl.pallas_call(
        functools.partial(_even_proj_kernel, with_rows=with_rows),
        grid=(bx, tx // tm), in_specs=in_specs, out_specs=out_specs, out_shape=out_shape,
        compiler_params=_cparams("parallel", "parallel"), name="even_proj")(*args)


def _odd_proj_kernel(*refs, with_rows):
    if with_rows:
        x_ref, g_ref, wn_ref, wt_ref, wr_ref, q_ref, kvT_ref, kvTb_ref, rows_ref = refs
    else:
        x_ref, g_ref, wn_ref, wt_ref, q_ref, kvT_ref, kvTb_ref = refs
    h = _rms(x_ref[0], g_ref[...]).astype(BF16)
    q_ref[0] = _dot(h, wn_ref[...]).astype(BF16)
    t = _dot_nt(wt_ref[...], h)
    kvT_ref[0] = t
    kvTb_ref[0] = t.astype(BF16)
    if with_rows:
        rows_ref[0] = _dot(h, wr_ref[...])


def _odd_proj(x, g, wn, wt, wr=None):
    bx, tx, d = x.shape
    tm = min(512, tx)
    with_rows = wr is not None
    const2 = lambda b, j: (0, 0)
    in_specs = [pl.BlockSpec((1, tm, d), lambda b, j: (b, j, 0)),
                pl.BlockSpec((1, d), const2),
                pl.BlockSpec(wn.shape, const2),
                pl.BlockSpec(wt.shape, const2)]
    args = [x, g, wn, wt]
    out_specs = [pl.BlockSpec((1, tm, 1024), lambda b, j: (b, j, 0)),
                 pl.BlockSpec((1, 256, tm), lambda b, j: (b, 0, j)),
                 pl.BlockSpec((1, 256, tm), lambda b, j: (b, 0, j))]
    out_shape = [jax.ShapeDtypeStruct((bx, tx, 1024), BF16),
                 jax.ShapeDtypeStruct((bx, 256, tx), F32),
                 jax.ShapeDtypeStruct((bx, 256, tx), BF16)]
    if with_rows:
        in_specs.append(pl.BlockSpec(wr.shape, const2))
        args.append(wr)
        out_specs.append(pl.BlockSpec((1, tm, 256), lambda b, j: (b, j, 0)))
        out_shape.append(jax.ShapeDtypeStruct((bx, tx, 256), F32))
    return pl.pallas_call(
        functools.partial(_odd_proj_kernel, with_rows=with_rows),
        grid=(bx, tx // tm), in_specs=in_specs, out_specs=out_specs, out_shape=out_shape,
        compiler_params=_cparams("parallel", "parallel"), name="odd_proj")(*args)


def _cumsum_kernel(lT_ref, sm_ref, kTb_ref, triu_ref, tril_ref, c_ref, kA_ref, carT_ref, carR_ref):
    @pl.when(pl.program_id(1) == 0)
    def _():
        carT_ref[...] = jnp.zeros_like(carT_ref)
        carR_ref[...] = jnp.zeros_like(carR_ref)

    tm = lT_ref.shape[-1]
    triu = triu_ref[...]
    hi, mid, lo = _split3(lT_ref[0])
    cT = _dot(hi, triu) + _dot(mid, triu) + _dot(lo, triu) + carT_ref[:, 0:1]
    carT_ref[...] = jnp.broadcast_to(cT[:, tm - 1:tm], carT_ref.shape)
    tril = tril_ref[...]
    hi, mid, lo = _split3(sm_ref[0])
    c = _dot(tril, hi) + _dot(tril, mid) + _dot(tril, lo) + carR_ref[0:1, :]
    c_ref[0] = c
    carR_ref[...] = jnp.broadcast_to(c[tm - 1:tm, :], carR_ref.shape)

    row = _iota((8, tm), 0)
    ones8 = jnp.where(row < 3, 1.0, 0.0)
    pad = jnp.zeros((40, tm), F32)

    def neg_pieces(v):
        hi, mid, lo = _split3(v)
        return jnp.where(row == 0, -hi.astype(F32), jnp.where(row == 1, -mid.astype(F32),
                                                              jnp.where(row == 2, -lo.astype(F32), 0.0)))

    kT = kTb_ref[0].astype(F32)
    for g in range(FOX_KV_HEADS):
        kA_ref[0, g] = jnp.concatenate(
            [kT[g * HEAD_DIM:(g + 1) * HEAD_DIM], ones8, neg_pieces(cT[2 * g:2 * g + 1]),
             neg_pieces(cT[2 * g + 1:2 * g + 2]), pad], axis=0).astype(BF16)


def _cumsum(logfT, sm, fkvTb):
    b, _, t = logfT.shape
    tm = min(512, t)
    i = np.arange(tm)
    triu = jnp.asarray(i[:, None] <= i[None, :], BF16)
    tril = jnp.asarray(i[:, None] >= i[None, :], BF16)
    return pl.pallas_call(
        _cumsum_kernel, grid=(b, t // tm),
        in_specs=[pl.BlockSpec((1, FOX_HEADS, tm), lambda b, j: (b, 0, j)),
                  pl.BlockSpec((1, tm, 128), lambda b, j: (b, j, 0)),
                  pl.BlockSpec((1, 256, tm), lambda b, j: (b, 0, j)),
                  pl.BlockSpec((tm, tm), lambda b, j: (0, 0)),
                  pl.BlockSpec((tm, tm), lambda b, j: (0, 0))],
        out_specs=[pl.BlockSpec((1, tm, 128), lambda b, j: (b, j, 0)),
                   pl.BlockSpec((1, FOX_KV_HEADS, 128, tm), lambda b, j: (b, 0, 0, j))],
        out_shape=[jax.ShapeDtypeStruct((b, t, 128), F32),
                   jax.ShapeDtypeStruct((b, FOX_KV_HEADS, 128, t), BF16)],
        scratch_shapes=[pltpu.VMEM((FOX_HEADS, 128), F32), pltpu.VMEM((8, 128), F32)],
        compiler_params=_cparams("parallel", "arbitrary"), name="fox_cumsum")(logfT, sm, fkvTb, triu, tril)


def _softmax_step(carry, s, vT, mask=None):
    m, l, acc = carry
    m_new = jnp.maximum(m, jnp.max(s, axis=-1, keepdims=True))
    p = jnp.exp(s - m_new)
    if mask is not None:
        p = jnp.where(mask, p, 0.0)
    alpha = jnp.exp(m - m_new)
    l = alpha * l + jnp.sum(p, axis=-1, keepdims=True)
    acc = alpha * acc + _dot_nt(p.astype(BF16), vT)
    return m_new, l, acc


def _softmax_first(s, vT, mask):
    m = jnp.max(s, axis=-1, keepdims=True)
    p = jnp.where(mask, jnp.exp(s - m), 0.0)
    return m, jnp.sum(p, axis=-1, keepdims=True), _dot_nt(p.astype(BF16), vT)


def _fox_attn_kernel(q_ref, c_ref, kA_ref, vT_ref, o_ref):
    g = pl.program_id(1)
    iq = pl.program_id(2)
    tq = q_ref.shape[1]
    q2 = q_ref[0]
    cblk = c_ref[0]
    lane = _iota((tq, 128), 1)
    one = jnp.ones((tq, 1), BF16)
    qa = []
    for r in range(2):
        hi, mid, lo = _split3(_lane_pick(cblk, 2 * g + r))
        base = 72 + 8 * r
        qa.append(jnp.where(lane == 64, hi, jnp.where(lane == 65, mid, jnp.where(
            lane == 66, lo, jnp.where((lane >= base) & (lane < base + 3), one, q2[:, r * 128:(r + 1) * 128])))))
    causal = _iota((tq, tq), 1) <= _iota((tq, tq), 0)

    def chunk(j):
        off = pl.multiple_of(j * tq, tq)
        return kA_ref[0, 0, :, pl.ds(off, tq)], vT_ref[0, :, pl.ds(off, tq)]

    kA, vT = chunk(iq)
    carry = tuple(_softmax_first(jnp.where(causal, _dot(qa[r], kA), NEG_INF), vT, causal) for r in range(2))

    def body(j, carry):
        kA, vT = chunk(j)
        return tuple(_softmax_step(carry[r], _dot(qa[r], kA), vT) for r in range(2))

    carry = lax.fori_loop(0, iq, body, carry)
    o_ref[0] = jnp.concatenate([acc / jnp.maximum(l, TINY) for (_, l, acc) in carry], axis=-1).astype(BF16)


def _fox_attn(q, c, kA, fkvTb):
    b, t, _ = q.shape
    tq = min(FOX_TQ, t)
    return pl.pallas_call(
        _fox_attn_kernel, grid=(b, FOX_KV_HEADS, t // tq),
        in_specs=[pl.BlockSpec((1, tq, 256), lambda b, g, i: (b, i, g)),
                  pl.BlockSpec((1, tq, 128), lambda b, g, i: (b, i, 0)),
                  pl.BlockSpec((1, 1, 128, t), lambda b, g, i: (b, g, 0, 0)),
                  pl.BlockSpec((1, HEAD_DIM, t), lambda b, g, i: (b, FOX_KV_HEADS + g, 0))],
        out_specs=pl.BlockSpec((1, tq, 128), lambda b, g, i: (b, i, g)),
        out_shape=jax.ShapeDtypeStruct((b, t, 512), BF16),
        compiler_params=_cparams("parallel", "parallel", "arbitrary"), name="fox_attn")(q, c, kA, fkvTb)


def _cmp_prompt_kernel(x_ref, pool_ref, peT_ref, wT_ref, featT_ref, kcT_ref, kc_ref):
    x = x_ref[0]
    hi = x.astype(BF16)
    lo = (x - hi.astype(F32)).astype(BF16)
    pool = pool_ref[...]
    meanT = _dot(hi, pool) + _dot(lo, pool)
    meanT = meanT + jnp.mean(peT_ref[...], axis=-1, keepdims=True)
    kcT = _dot(wT_ref[...], meanT.astype(BF16)) + featT_ref[...]
    kcT_ref[0] = kcT.astype(BF16)
    kc_ref[0] = kcT.T.astype(BF16)


def _cmp_columns(n_blocks):
    j = np.arange(n_blocks)
    return np.where(j % 2 == 0, j // 2, n_blocks // 2 + j // 2)


def _cmp_prompt(cmpT, peT, w_augT):
    b, _, t = cmpT.shape
    nc = t // NSA_CMP_BLOCK
    col_of_block = _cmp_columns(nc)
    cols = col_of_block[np.arange(t) // NSA_CMP_BLOCK]
    pool = jnp.asarray((cols[:, None] == np.arange(nc)[None, :]) / NSA_CMP_BLOCK, BF16)
    cpos = np.zeros(nc, np.int64)
    cpos[col_of_block] = NSA_CMP_BLOCK * (np.arange(nc) + 1) - 1
    featT = np.zeros((512, nc), np.float32)
    for g in range(NSA_KV_HEADS):
        featT[g * 128 + 64] = 1.0
        featT[g * 128 + 65] = 1.0
        featT[g * 128 + 66] = (cpos >> 7) << 7
        featT[g * 128 + 67] = cpos & 127
    return pl.pallas_call(
        _cmp_prompt_kernel, grid=(b,),
        in_specs=[pl.BlockSpec((1, 256, t), lambda b: (b, 0, 0)),
                  pl.BlockSpec((t, nc), lambda b: (0, 0)),
                  pl.BlockSpec((256, NSA_CMP_BLOCK), lambda b: (0, 0)),
                  pl.BlockSpec((512, 256), lambda b: (0, 0)),
                  pl.BlockSpec((512, nc), lambda b: (0, 0))],
        out_specs=[pl.BlockSpec((1, 512, nc), lambda b: (b, 0, 0)),
                   pl.BlockSpec((1, nc, 512), lambda b: (b, 0, 0))],
        out_shape=[jax.ShapeDtypeStruct((b, 512, nc), BF16), jax.ShapeDtypeStruct((b, nc, 512), BF16)],
        compiler_params=_cparams("parallel"), name="nsa_compress")(cmpT, pool, peT, w_augT, jnp.asarray(featT))


def _topk_select(val, idx_f, n, n_top, axis):
    sel = jnp.zeros_like(val)
    for _ in range(n_top):
        mx = jnp.max(val, axis=axis, keepdims=True)
        idx = jnp.min(jnp.where(val == mx, idx_f, float(n)), axis=axis, keepdims=True)
        hit = idx_f == idx
        sel = jnp.where(hit, 1.0, sel)
        val = jnp.where(hit, -2.0, val)
    return sel


def _nsa_attn_kernel(q_ref, sm_ref, kcT_ref, vcT_ref, kc_ref, ks_ref, vs_ref, kw_ref, vw_ref, eye_ref, o_ref,
                     live_ref, *, tk, nwin):
    g = pl.program_id(1)
    iq = pl.program_id(2)
    tq = q_ref.shape[1]
    nc = kcT_ref.shape[-1]
    ns = nc // 2
    s0 = iq * tq
    tpos = s0 + _iota((tq, 1), 0)
    tpos4 = s0 + jnp.bitwise_and(_iota((4 * tq, 1), 0), tq - 1)
    tposT = s0 + _iota((1, tq), 1)
    q4 = q_ref[0]
    gates = sm_ref[0]
    qs = jnp.concatenate(
        [_alibi_query(q4[:, r * 128:(r + 1) * 128], tpos, jnp.where(g == 0, NSA_SLOPES[r], NSA_SLOPES[4 + r]))
         for r in range(4)], axis=0)

    def cmp_pos(col):
        j = jnp.where(col < ns, 2 * col, 2 * (col - ns) + 1)
        return NSA_CMP_BLOCK * (j + 1) - 1

    mask_c = tpos4 >= cmp_pos(_iota((1, nc), 1))
    s = jnp.where(mask_c, _dot(qs, kcT_ref[0]), NEG_INF)
    m = jnp.max(s, axis=-1, keepdims=True)
    p = jnp.where(mask_c, jnp.exp(s - m), 0.0)
    p = p / jnp.maximum(jnp.sum(p, axis=-1, keepdims=True), TINY)
    o_cmp = _dot_nt(p.astype(BF16), vcT_ref[0, :HEAD_DIM, :])

    sT = _dot_nt(kc_ref[0], qs)
    maskT = tposT >= cmp_pos(_iota((nc, 1), 0))
    imp = None
    for r in range(4):
        sr = jnp.where(maskT, sT[:, r * tq:(r + 1) * tq], NEG_INF)
        mr = jnp.max(sr, axis=0, keepdims=True)
        pr = jnp.where(maskT, jnp.exp(sr - mr), 0.0)
        pr = pr / jnp.maximum(jnp.sum(pr, axis=0, keepdims=True), TINY)
        imp = pr if imp is None else imp + pr
    imp = imp[:ns] + imp[ns:]
    blk = _iota((ns, 1), 0)
    cur = jnp.right_shift(tposT, 6)
    visible = blk * NSA_SEL_BLOCK <= tposT
    forced = (blk == 0) | (blk == cur) | (blk == cur - 1)
    val = jnp.where(visible, jnp.where(forced, FORCED_SCORE, imp), -1.0)
    sel = _topk_select(val, blk.astype(F32), ns, min(NSA_TOPK, ns), 0)
    selbT = jnp.where((sel > 0.0) & visible, 0.0, NEG_INF).astype(BF16)
    selb = _dot_nt(eye_ref[...], selbT).astype(BF16)
    q2 = jnp.concatenate([qs, jnp.concatenate([selb] * 4, axis=0)], axis=1)

    def sel_chunk(j):
        off = pl.multiple_of(j * tk, tk)
        return ks_ref[0, 0, :, pl.ds(off, tk)], vs_ref[0, :, pl.ds(off, tk)], off

    n_full = s0 // tk
    kA, vT, off = sel_chunk(n_full)
    ok = (off + _iota((1, tk), 1)) <= tpos4
    carry = _softmax_first(jnp.where(ok, _dot(q2, kA), NEG_INF), vT, ok)

    picked = jnp.max(jnp.where((sel > 0.0) & visible, 1.0, 0.0), axis=-1, keepdims=True)
    per_chunk = tk // NSA_SEL_BLOCK
    n_live = jnp.int32(0)
    for j in range(ns // per_chunk):
        live = (jnp.max(picked[j * per_chunk:(j + 1) * per_chunk]) > 0.0) & (j < n_full)
        live_ref[n_live] = j
        n_live = n_live + live.astype(jnp.int32)

    def body(i, carry):
        kA, vT, _ = sel_chunk(live_ref[i])
        return _softmax_step(carry, _dot(q2, kA), vT)

    _, l, acc = lax.fori_loop(0, n_live, body, carry)
    o_sel = acc / jnp.maximum(l, TINY)

    start = pl.multiple_of(jnp.maximum(s0 - NSA_WINDOW, 0), 128)
    dist = tpos4 - (start + _iota((1, nwin), 1))
    ok = (dist >= 0) & (dist <= NSA_WINDOW)
    s = jnp.where(ok, _dot(qs, kw_ref[0, 0, :, pl.ds(start, nwin)]), NEG_INF)
    _, l, acc = _softmax_first(s, vw_ref[0, :, pl.ds(start, nwin)], ok)
    o_win = acc / jnp.maximum(l, TINY)

    outs = []
    for r in range(4):
        rows = slice(r * tq, (r + 1) * tq)
        base = FOX_HEADS + (g * 4 + r) * 3
        outs.append(_lane_pick(gates, base) * o_cmp[rows] + _lane_pick(gates, base + 1) * o_sel[rows]
                    + _lane_pick(gates, base + 2) * o_win[rows])
    o_ref[0] = jnp.concatenate(outs, axis=-1).astype(BF16)


def _nsa_attn(q, sm, kcT, kc, kselA, selTb, kwinA, winTb):
    b, t, _ = q.shape
    tq = min(NSA_TQ, t)
    tk = min(512, t)
    nwin = min(NSA_WINDOW + tq, t)
    nc = kcT.shape[-1]
    ka = kselA.shape[2]
    eye = jnp.asarray(np.eye(tq), BF16)
    return pl.pallas_call(
        functools.partial(_nsa_attn_kernel, tk=tk, nwin=nwin), grid=(b, NSA_KV_HEADS, t // tq),
        in_specs=[pl.BlockSpec((1, tq, 512), lambda b, g, i: (b, i, 2 + g)),
                  pl.BlockSpec((1, tq, 128), lambda b, g, i: (b, i, 0)),
                  pl.BlockSpec((1, 128, nc), lambda b, g, i: (b, g, 0)),
                  pl.BlockSpec((1, 128, nc), lambda b, g, i: (b, 2 + g, 0)),
                  pl.BlockSpec((1, nc, 128), lambda b, g, i: (b, 0, g)),
                  pl.BlockSpec((1, 1, ka, t), lambda b, g, i: (b, g, 0, 0)),
                  pl.BlockSpec((1, HEAD_DIM, t), lambda b, g, i: (b, 2 + g, 0)),
                  pl.BlockSpec((1, 1, 128, t), lambda b, g, i: (b, g, 0, 0)),
                  pl.BlockSpec((1, HEAD_DIM, t), lambda b, g, i: (b, 2 + g, 0)),
                  pl.BlockSpec((tq, tq), lambda b, g, i: (0, 0))],
        out_specs=pl.BlockSpec((1, tq, 256), lambda b, g, i: (b, i, g)),
        out_shape=jax.ShapeDtypeStruct((b, t, 512), BF16),
        scratch_shapes=[pltpu.SMEM((t // tk + 1,), jnp.int32)],
        compiler_params=_cparams("parallel", "parallel", "arbitrary"), name="nsa_attn")(
            q, sm, kcT, kcT, kc, kselA, selTb, kwinA, winTb, eye)


def _swa_attn_kernel(q_ref, kT_ref, vT_ref, sink_ref, o_ref, *, nkeys):
    g = pl.program_id(1)
    iq = pl.program_id(2)
    tq = q_ref.shape[1]
    s0 = iq * tq
    tpos = s0 + _iota((tq, 1), 0)
    start = pl.multiple_of(jnp.maximum(s0 - SWA_WINDOW, 0), 128)
    kT = kT_ref[0, :, pl.ds(start, nkeys)]
    vT = vT_ref[0, :, pl.ds(start, nkeys)]
    dist = tpos - (start + _iota((1, nkeys), 1))
    ok = (dist >= 0) & (dist <= SWA_WINDOW)
    distf = dist.astype(F32)
    q8 = q_ref[0]
    sinks = sink_ref[...]
    outs = []
    for r in range(8):
        slope = jnp.where(g == 0, SWA_SLOPES[r], SWA_SLOPES[8 + r])
        sink = _lane_pick(sinks, g * 8 + r)
        s = jnp.where(ok, _dot(q8[:, r * HEAD_DIM:(r + 1) * HEAD_DIM], kT) - slope * distf, NEG_INF)
        m = jnp.maximum(jnp.max(s, axis=-1, keepdims=True), sink)
        p = jnp.where(ok, jnp.exp(s - m), 0.0)
        den = jnp.sum(p, axis=-1, keepdims=True) + jnp.exp(sink - m)
        outs.append(_dot_nt(p.astype(BF16), vT) / jnp.maximum(den, TINY))
    o_ref[0] = jnp.concatenate(outs, axis=-1).astype(BF16)


def _swa_attn(q, kvTb, sinks):
    b, t, _ = q.shape
    tq = 128
    nkeys = min(SWA_WINDOW + tq, t)
    return pl.pallas_call(
        functools.partial(_swa_attn_kernel, nkeys=nkeys), grid=(b, SWA_KV_HEADS, t // tq),
        in_specs=[pl.BlockSpec((1, tq, 512), lambda b, g, i: (b, i, g)),
                  pl.BlockSpec((1, HEAD_DIM, t), lambda b, g, i: (b, g, 0)),
                  pl.BlockSpec((1, HEAD_DIM, t), lambda b, g, i: (b, 2 + g, 0)),
                  pl.BlockSpec((1, 128), lambda b, g, i: (0, 0))],
        out_specs=pl.BlockSpec((1, tq, 512), lambda b, g, i: (b, i, g)),
        out_shape=jax.ShapeDtypeStruct((b, t, 1024), BF16),
        compiler_params=_cparams("parallel", "parallel", "arbitrary"), name="swa_attn")(q, kvTb, kvTb, sinks)


def _out_ffn_kernel(*refs, n_mix, final):
    x_ref = refs[0]
    o_refs = refs[1:1 + n_mix]
    wo_ref, gf_ref, wg_ref, wu_ref, wd_ref = refs[1 + n_mix:6 + n_mix]
    gfin_ref = refs[6 + n_mix] if final else None
    y_ref = refs[-1]
    mix = [o_ref[...] for o_ref in o_refs]
    mix = mix[0] if n_mix == 1 else jnp.concatenate(mix, axis=-1)
    x = x_ref[...] + _dot(mix, wo_ref[...])
    h = _rms(x, gf_ref[...]).astype(BF16)
    gt = _dot(h, wg_ref[...])
    up = _dot(h, wu_ref[...])
    x = x + _dot((jax.nn.silu(gt) * up).astype(BF16), wd_ref[...])
    if final:
        x = _rms(x, gfin_ref[...])
    y_ref[...] = x


def _out_ffn(x, mixes, wo, gf, wg, wu, wd, gfin=None):
    rows, d = x.shape
    tm = min(512, rows)
    final = gfin is not None
    const = lambda i: (0, 0)
    resident = lambda shape: pl.BlockSpec(shape, const, pipeline_mode=pl.Buffered(1))
    in_specs = [pl.BlockSpec((tm, d), lambda i: (i, 0))]
    in_specs += [pl.BlockSpec((tm, m.shape[1]), lambda i: (i, 0)) for m in mixes]
    in_specs += [resident(wo.shape), pl.BlockSpec((1, d), const), resident(wg.shape), resident(wu.shape),
                 resident(wd.shape)]
    args = [x, *mixes, wo, gf, wg, wu, wd]
    if final:
        in_specs.append(pl.BlockSpec((1, d), const))
        args.append(gfin)
    return pl.pallas_call(
        functools.partial(_out_ffn_kernel, n_mix=len(mixes), final=final),
        grid=(rows // tm,), in_specs=in_specs,
        out_specs=pl.BlockSpec((tm, d), lambda i: (i, 0)),
        out_shape=jax.ShapeDtypeStruct((rows, d), F32),
        compiler_params=_cparams("parallel"), name="out_ffn")(*args)


def _fox_decode_kernel(pt_ref, qbd_ref, lfn_ref, knew_ref, vnew_ref, sfx_ref, *refs, pps):
    kv_refs = refs[:pps]
    lf_refs = refs[pps:2 * pps]
    o_ref = refs[2 * pps]
    m_ref, l_ref, acc_ref, car_ref = refs[2 * pps + 1:]
    c = pl.program_id(1)
    qbd = qbd_ref[...]

    @pl.when(c == 0)
    def _():
        s_new = jnp.sum(qbd.astype(F32) * knew_ref[...], axis=-1, keepdims=True)
        m_ref[...] = jnp.broadcast_to(s_new, m_ref.shape)
        l_ref[...] = jnp.ones_like(l_ref)
        acc_ref[...] = jnp.broadcast_to(vnew_ref[...], acc_ref.shape)
        car_ref[...] = lfn_ref[...]

    m = m_ref[:, 0:1]
    carry = car_ref[:, 0:1]
    lf_all = jnp.concatenate([lf_refs[i][...] for i in range(pps)], axis=0)
    sfx = sfx_ref[...]
    hi, mid, lo = _split3(lf_all)
    z_all = _dot(hi, sfx) + _dot(mid, sfx) + _dot(lo, sfx)
    tot_all = jnp.sum(lf_all, axis=-1, keepdims=True)
    scores = []
    for i in range(pps):
        rows = slice(FOX_HEADS * i, FOX_HEADS * (i + 1))
        scores.append(_dot(qbd, kv_refs[i][0].astype(BF16)) + (z_all[rows] + carry))
        carry = carry + tot_all[rows]
    mx = scores[0]
    for s in scores[1:]:
        mx = jnp.maximum(mx, s)
    m_new = jnp.maximum(m, jnp.max(mx, axis=-1, keepdims=True))
    alpha = jnp.exp(m - m_new)
    psum = None
    pv = None
    for i in range(pps):
        p = jnp.exp(scores[i] - m_new)
        part = _dot_nt(p.astype(BF16), kv_refs[i][1].astype(BF16))
        psum = p if psum is None else psum + p
        pv = part if pv is None else pv + part
    l = alpha * l_ref[:, 0:1] + jnp.sum(psum, axis=-1, keepdims=True)
    acc = alpha * acc_ref[...] + pv
    m_ref[...] = jnp.broadcast_to(m_new, m_ref.shape)
    l_ref[...] = jnp.broadcast_to(l, l_ref.shape)
    acc_ref[...] = acc
    car_ref[...] = jnp.broadcast_to(carry, car_ref.shape)

    @pl.when(c == pl.num_programs(1) - 1)
    def _():
        o_ref[...] = acc / jnp.maximum(l, TINY)


def _pages_per_step(n_pages, want):
    pps = want
    while n_pages % pps:
        pps //= 2
    return pps


def _fox_decode(page_table, qbd, lfn, knew, vnew, kv_pages, lf_pages):
    nb, npg = page_table.shape
    pps = _pages_per_step(npg, 64)

    def page_map(i):
        return lambda b, c, pt: (pt[b, npg - 1 - (c * pps + i)], 0, 0, 0)

    def lf_map(i):
        return lambda b, c, pt: (pt[b, npg - 1 - (c * pps + i)], 0, 0)

    per_b3 = lambda b, c, pt: (b, 0, 0)
    r = np.arange(PAGE_SIZE)
    sfx = jnp.asarray(r[:, None] > r[None, :], BF16)
    in_specs = [pl.BlockSpec((None, FOX_HEADS, 256), per_b3),
                pl.BlockSpec((None, FOX_HEADS, 128), per_b3),
                pl.BlockSpec((None, 1, 256), per_b3),
                pl.BlockSpec((None, 1, 256), per_b3),
                pl.BlockSpec((PAGE_SIZE, PAGE_SIZE), lambda b, c, pt: (0, 0))]
    in_specs += [pl.BlockSpec((None, 2, 256, PAGE_SIZE), page_map(i)) for i in range(pps)]
    in_specs += [pl.BlockSpec((None, FOX_HEADS, PAGE_SIZE), lf_map(i)) for i in range(pps)]
    grid_spec = pltpu.PrefetchScalarGridSpec(
        num_scalar_prefetch=1, grid=(nb, npg // pps), in_specs=in_specs,
        out_specs=pl.BlockSpec((None, FOX_HEADS, 256), per_b3),
        scratch_shapes=[pltpu.VMEM((FOX_HEADS, 128), F32), pltpu.VMEM((FOX_HEADS, 128), F32),
                        pltpu.VMEM((FOX_HEADS, 256), F32), pltpu.VMEM((FOX_HEADS, 128), F32)])
    return pl.pallas_call(
        functools.partial(_fox_decode_kernel, pps=pps), grid_spec=grid_spec,
        out_shape=jax.ShapeDtypeStruct((nb, FOX_HEADS, 256), F32),
        compiler_params=_cparams("parallel", "arbitrary"), name="fox_decode")(
            page_table, qbd, lfn, knew, vnew, sfx, *([kv_pages] * pps), *([lf_pages] * pps))


def _cmp_decode_kernel(pt_ref, pool_ref, pe_ref, w_ref, *refs, pps):
    x_refs = refs[:pps]
    kc_ref = refs[pps]
    mean = None
    for i in range(pps):
        part = _dot_nt(pool_ref[i], x_refs[i][...].astype(BF16))
        mean = part if mean is None else mean + part
    mean = mean + jnp.mean(pe_ref[...], axis=0, keepdims=True)
    summ = _dot(mean.astype(BF16), w_ref[...])
    half = 2 * pps
    kc_ref[0] = summ[:half]
    kc_ref[1] = summ[half:]


def _cmp_decode(page_table, cmp_pages, pe_rows, wbd):
    nb, npg = page_table.shape
    pps = _pages_per_step(npg, 32)
    per_page = PAGE_SIZE // NSA_CMP_BLOCK
    lane_blk = np.arange(PAGE_SIZE) // NSA_CMP_BLOCK
    pool = np.zeros((pps, per_page * pps, PAGE_SIZE), np.float32)
    for i in range(pps):
        row = (lane_blk % 2) * (2 * pps) + 2 * i + lane_blk // 2
        pool[i, row, np.arange(PAGE_SIZE)] = 1.0 / NSA_CMP_BLOCK
    pool = jnp.asarray(pool, BF16)

    def page_map(i):
        return lambda b, c, pt: (pt[b, c * pps + i], 0, 0)

    in_specs = [pl.BlockSpec(pool.shape, lambda b, c, pt: (0, 0, 0)),
                pl.BlockSpec(pe_rows.shape, lambda b, c, pt: (0, 0)),
                pl.BlockSpec((256, 256), lambda b, c, pt: (0, 0))]
    in_specs += [pl.BlockSpec((None, 256, PAGE_SIZE), page_map(i)) for i in range(pps)]
    n_half = 2 * npg
    grid_spec = pltpu.PrefetchScalarGridSpec(
        num_scalar_prefetch=1, grid=(nb, npg // pps), in_specs=in_specs,
        out_specs=pl.BlockSpec((None, 2, 2 * pps, 256), lambda b, c, pt: (b, 0, c, 0)))
    return pl.pallas_call(
        functools.partial(_cmp_decode_kernel, pps=pps), grid_spec=grid_spec,
        out_shape=jax.ShapeDtypeStruct((nb, 2, n_half, 256), F32),
        compiler_params=_cparams("parallel", "arbitrary"), name="nsa_cmp_decode")(
            page_table, pool, pe_rows, wbd, *([cmp_pages] * pps))


def _nsa_select_kernel(qbd_ref, kc_ref, slope_ref, ocmp_ref, idx_ref, *, past, n_pick):
    n_seq = qbd_ref.shape[0]
    n_half = kc_ref.shape[2]
    nc = 2 * n_half
    lane = _iota((1, nc), 1)
    j_orig = jnp.where(lane < n_half, 2 * lane, 2 * (lane - n_half) + 1)
    dist = past - (NSA_CMP_BLOCK * (j_orig + 1) - 1)
    ok = dist >= 0
    bias = slope_ref[:, 0:1] * dist.astype(F32)
    blk = _iota((1, n_half), 1)
    blk_f = blk.astype(F32)
    slot = _iota((1, NSA_TOPK), 1)
    forced = (blk == 0) | (blk == n_half - 1)
    for u in range(n_seq):
        kc = kc_ref[u].reshape(nc, 256)
        s = jnp.where(ok, _dot_nt(qbd_ref[u], kc[:, :128].astype(BF16)) - bias, NEG_INF)
        m = jnp.max(s, axis=-1, keepdims=True)
        p = jnp.where(ok, jnp.exp(s - m), 0.0)
        p = p / jnp.maximum(jnp.sum(p, axis=-1, keepdims=True), TINY)
        ocmp_ref[u] = _dot(p.astype(BF16), kc[:, 128:].astype(BF16))
        for g in range(NSA_KV_HEADS):
            imp = p[4 * g:4 * g + 1] + p[4 * g + 1:4 * g + 2] + p[4 * g + 2:4 * g + 3] + p[4 * g + 3:4 * g + 4]
            val = jnp.where(forced, FORCED_SCORE, imp[:, :n_half] + imp[:, n_half:])
            picks = jnp.full((1, NSA_TOPK), n_half, jnp.int32)
            for it in range(n_pick):
                mx = jnp.max(val, axis=-1, keepdims=True)
                idx = jnp.min(jnp.where(val == mx, blk_f, float(n_half)), axis=-1, keepdims=True)
                val = jnp.where(blk_f == idx, -2.0, val)
                picks = jnp.where(slot == it, idx.astype(jnp.int32), picks)
            idx_ref[u, pl.ds(g, 1), :] = picks


def _nsa_select(qbd, kc, slopes, past):
    nb, _, n_half, _ = kc.shape
    n_pick = min(NSA_TOPK, n_half + 1) - 1
    n_seq = 8 if nb % 8 == 0 else 1
    return pl.pallas_call(
        functools.partial(_nsa_select_kernel, past=past, n_pick=n_pick), grid=(nb // n_seq,),
        in_specs=[pl.BlockSpec((n_seq, NSA_HEADS, 128), lambda b: (b, 0, 0)),
                  pl.BlockSpec((n_seq, 2, n_half, 256), lambda b: (b, 0, 0, 0)),
                  pl.BlockSpec((NSA_HEADS, 128), lambda b: (0, 0))],
        out_specs=[pl.BlockSpec((n_seq, NSA_HEADS, 128), lambda b: (b, 0, 0)),
                   pl.BlockSpec((n_seq, NSA_KV_HEADS, NSA_TOPK), lambda b: (b, 0, 0))],
        out_shape=[jax.ShapeDtypeStruct((nb, NSA_HEADS, 128), F32),
                   jax.ShapeDtypeStruct((nb, NSA_KV_HEADS, NSA_TOPK), jnp.int32)],
        compiler_params=_cparams("parallel"), name="nsa_select")(qbd, kc, slopes)


def _shift_in(state, new_col):
    n = state.shape[-1]
    lane = _iota(state.shape, 1)
    return jnp.where(lane < n - 1, pltpu.roll(state, n - 1, axis=1), new_col)


def _nsa_decode_kernel(pt_ref, idx_ref, q_ref, new_ref, gate_ref, ocmp_ref, slope_ref, win_ref, *refs,
                       past, n_pick):
    pages = refs[:n_pick]
    o_ref, wout_ref = refs[n_pick:]
    b = pl.program_id(0)
    g = pl.program_id(1)
    q = q_ref[...]
    qf = q.astype(F32)
    new = new_ref[...]
    newb = new.astype(BF16).astype(F32)
    slope = slope_ref[:, 0:1]
    lane = _iota((1, PAGE_SIZE), 1)

    m = jnp.sum(qf * newb[0:1], axis=-1, keepdims=True)
    l = jnp.ones_like(m)
    acc = jnp.broadcast_to(newb[1:2], (8, HEAD_DIM))
    for i in range(n_pick):
        blk = idx_ref[b * NSA_KV_HEADS + g, i]
        half = blk % 2
        ok = jnp.right_shift(lane, 6) == half
        pos = blk * NSA_SEL_BLOCK + lane - NSA_SEL_BLOCK * half
        s = _dot(q, pages[i][0].astype(BF16)) - slope * (past - pos).astype(F32)
        m, l, acc = _softmax_step((m, l, acc), jnp.where(ok, s, NEG_INF), pages[i][1].astype(BF16), ok)
    o_sel = acc / jnp.maximum(l, TINY)

    n_buf = win_ref.shape[-1]
    kT = win_ref[0]
    vT = win_ref[1]
    dist = (n_buf - _iota((1, n_buf), 1)).astype(F32)
    m = jnp.sum(qf * newb[2:3], axis=-1, keepdims=True)
    carry = (m, jnp.ones_like(m), jnp.broadcast_to(newb[3:4], (8, HEAD_DIM)))
    _, l, acc = _softmax_step(carry, _dot(q, kT.astype(BF16)) - slope * dist, vT.astype(BF16))
    o_win = acc / jnp.maximum(l, TINY)

    gate = gate_ref[...]
    o_ref[...] = gate[:, 0:1] * ocmp_ref[...] + gate[:, 1:2] * o_sel + gate[:, 2:3] * o_win
    wout_ref[0] = _shift_in(kT, _row_to_col(new[2:3]))
    wout_ref[1] = _shift_in(vT, _row_to_col(new[3:4]))


def _nsa_decode(page_table, idx, q, new, gate, ocmp, slopes, win_state, sel_pages, past):
    nb, npg = page_table.shape
    n_buf = win_state.shape[-1]
    n_pick = min(NSA_TOPK, 2 * npg + 1) - 1
    idx2 = idx.reshape(nb * NSA_KV_HEADS, NSA_TOPK)

    def page_map(i):
        return lambda b, g, pt, ix: (pt[b, ix[b * NSA_KV_HEADS + g, i] // 2], 0, g, 0, 0)

    per_bg = lambda b, g, pt, ix: (b, g, 0, 0)
    in_specs = [pl.BlockSpec((None, None, 8, HEAD_DIM), per_bg),
                pl.BlockSpec((None, None, 4, HEAD_DIM), per_bg),
                pl.BlockSpec((None, None, 8, 128), per_bg),
                pl.BlockSpec((None, None, 8, HEAD_DIM), per_bg),
                pl.BlockSpec((None, 8, 128), lambda b, g, pt, ix: (g, 0, 0)),
                pl.BlockSpec((None, 2, None, HEAD_DIM, n_buf), lambda b, g, pt, ix: (b, 0, g, 0, 0))]
    in_specs += [pl.BlockSpec((None, 2, None, HEAD_DIM, PAGE_SIZE), page_map(i)) for i in range(n_pick)]
    grid_spec = pltpu.PrefetchScalarGridSpec(
        num_scalar_prefetch=2, grid=(nb, NSA_KV_HEADS), in_specs=in_specs,
        out_specs=[pl.BlockSpec((None, None, 8, HEAD_DIM), per_bg),
                   pl.BlockSpec((None, 2, None, HEAD_DIM, n_buf), lambda b, g, pt, ix: (b, 0, g, 0, 0))])
    return pl.pallas_call(
        functools.partial(_nsa_decode_kernel, past=past, n_pick=n_pick), grid_spec=grid_spec,
        out_shape=[jax.ShapeDtypeStruct((nb, NSA_KV_HEADS, 8, HEAD_DIM), F32),
                   jax.ShapeDtypeStruct(win_state.shape, F32)],
        compiler_params=_cparams("parallel", "arbitrary"), name="nsa_decode")(
            page_table, idx2, q, new, gate, ocmp, slopes, win_state, *([sel_pages] * n_pick))


def _swa_decode_kernel(qbd_ref, new_ref, sink_ref, slope_ref, st_ref, o_ref, sout_ref):
    n_buf = st_ref.shape[-1]
    qbd = qbd_ref[...]
    new = new_ref[...]
    newb = new.astype(BF16).astype(F32)
    kT = st_ref[0].reshape(128, n_buf)
    vT = st_ref[1].reshape(128, n_buf)
    dist = (n_buf - _iota((1, n_buf), 1)).astype(F32)
    s = _dot(qbd, kT.astype(BF16)) - slope_ref[:, 0:1] * dist
    s_new = jnp.sum(qbd.astype(F32) * newb[:, :128], axis=-1, keepdims=True)
    sink = sink_ref[:, 0:1]
    m = jnp.maximum(jnp.maximum(jnp.max(s, axis=-1, keepdims=True), s_new), sink)
    p = jnp.exp(s - m)
    p_new = jnp.exp(s_new - m)
    den = jnp.sum(p, axis=-1, keepdims=True) + p_new + jnp.exp(sink - m)
    acc = _dot_nt(p.astype(BF16), vT.astype(BF16)) + p_new.astype(BF16).astype(F32) * newb[:, 128:]
    o_ref[...] = acc / jnp.maximum(den, TINY)
    sout_ref[0] = _shift_in(kT, _row_to_col(new[:, :128])).reshape(st_ref.shape[1:])
    sout_ref[1] = _shift_in(vT, _row_to_col(new[:, 128:])).reshape(st_ref.shape[1:])


def _swa_decode(qbd, new, sinks, slopes, state):
    nb = qbd.shape[0]
    n_buf = state.shape[-1]
    st_spec = pl.BlockSpec((None, 2, SWA_KV_HEADS, HEAD_DIM, n_buf), lambda b: (b, 0, 0, 0, 0))
    return pl.pallas_call(
        _swa_decode_kernel, grid=(nb,),
        in_specs=[pl.BlockSpec((None, SWA_HEADS, 128), lambda b: (b, 0, 0)),
                  pl.BlockSpec((None, 1, 256), lambda b: (b, 0, 0)),
                  pl.BlockSpec((SWA_HEADS, 128), lambda b: (0, 0)),
                  pl.BlockSpec((SWA_HEADS, 128), lambda b: (0, 0)),
                  st_spec],
        out_specs=[pl.BlockSpec((None, SWA_HEADS, 128), lambda b: (b, 0, 0)), st_spec],
        out_shape=[jax.ShapeDtypeStruct((nb, SWA_HEADS, 128), F32),
                   jax.ShapeDtypeStruct(state.shape, F32)],
        compiler_params=_cparams("parallel"), name="swa_decode")(qbd, new, sinks, slopes, state)


def _block_diag_q(q, n_groups, per_group):
    nb = q.shape[0]
    q5 = q.reshape(nb, n_groups, per_group, 1, HEAD_DIM)
    eye = jnp.eye(n_groups, dtype=q.dtype).reshape(1, n_groups, 1, n_groups, 1)
    return (q5 * eye).reshape(nb, n_groups * per_group, n_groups * HEAD_DIM)


def _block_diag_take(o, n_groups, per_group):
    nb = o.shape[0]
    o5 = o.reshape(nb, n_groups, per_group, n_groups, HEAD_DIM)
    picked = jnp.stack([o5[:, g, :, g, :] for g in range(n_groups)], axis=1)
    return picked.reshape(nb, n_groups * per_group * HEAD_DIM)


def _t_last_view(x):
    nd = x.ndim
    return jnp.transpose(x, (0,) + tuple(range(2, nd)) + (1,))


def _t_first_view(x):
    nd = x.ndim
    return jnp.transpose(x, (0, nd - 1) + tuple(range(1, nd - 1)))


def kernel(x_prompt, x_sample, cache_fox_kv, cache_fox_logf, cache_nsa_cmp_kv, cache_nsa_sel_kv, state_nsa_win,
           state_swa, page_table, even_w_in, even_b_forget, even_pe_cmp, even_w_cmp, even_w_out, odd_w_in,
           odd_sinks, odd_w_out, norm_mix, norm_ffn, ffn_w_gate, ffn_w_up, ffn_w_down, norm_final):
    b, t, d = x_prompt.shape
    nb = x_sample.shape[0]
    npg = page_table.shape[1]
    past = npg * PAGE_SIZE
    assert norm_mix.shape[0] == 2 and x_sample.shape[1] == 1
    assert past >= NSA_WINDOW and t % 512 == 0

    w = even_w_in[0]
    fq, fkv, fl = w[:, 0:512], w[:, 512:1024], w[:, 1024:1032]
    nq, nkv, ng = w[:, 1032:1544], w[:, 1544:2312], w[:, 2312:2336]

    def pad_heads(wq):
        wq3 = wq.reshape(d, -1, HEAD_DIM)
        return jnp.concatenate([wq3, jnp.zeros_like(wq3)], axis=2).reshape(d, -1)

    e_wn = jnp.concatenate([pad_heads(fq * ATTN_SCALE), pad_heads(nq * ATTN_SCALE), fl, ng,
                            jnp.zeros((d, 96), F32)], axis=1).astype(BF16)
    e_wt = jnp.concatenate([fkv, nkv, fl], axis=1).T.astype(BF16)
    e_wr = jnp.concatenate([fkv, nkv], axis=1).astype(BF16)
    bsm = jnp.zeros((1, 128), F32).at[0, :FOX_HEADS].set(even_b_forget[0])
    bcol = even_b_forget[0].reshape(FOX_HEADS, 1)
    w_cmp = even_w_cmp[0].reshape(4, HEAD_DIM, HEAD_DIM)
    wbd = jnp.zeros((256, 256), F32)
    for i in range(4):
        wbd = wbd.at[i * 64:(i + 1) * 64, i * 64:(i + 1) * 64].set(w_cmp[i])
    wbd_b = wbd.astype(BF16)
    w_aug = jnp.concatenate([wbd.reshape(256, 4, HEAD_DIM), jnp.zeros((256, 4, HEAD_DIM), F32)], axis=2)
    w_augT = w_aug.reshape(256, 512).T.astype(BF16)
    pe_rows = even_pe_cmp[0].reshape(NSA_CMP_BLOCK, 256)
    o_w = odd_w_in[0]
    o_wn = (o_w[:, :1024] * ATTN_SCALE).astype(BF16)
    o_wt = o_w[:, 1024:].T.astype(BF16)
    o_wr = o_w[:, 1024:].astype(BF16)
    e_wo = even_w_out[0].astype(BF16)
    o_wo = odd_w_out[0].astype(BF16)
    wg = ffn_w_gate.astype(BF16)
    wu = ffn_w_up.astype(BF16)
    wd = ffn_w_down.astype(BF16)
    g_mix = norm_mix.reshape(2, 1, d)
    g_ffn = norm_ffn.reshape(2, 1, d)
    g_fin = norm_final.reshape(1, d)
    sinks_row = jnp.zeros((1, 128), F32).at[0, :SWA_HEADS].set(odd_sinks[0])
    sinks_col = jnp.broadcast_to(odd_sinks[0].reshape(SWA_HEADS, 1), (SWA_HEADS, 128))
    nsa_slopes8 = jnp.broadcast_to(jnp.asarray(NSA_SLOPES, F32).reshape(NSA_HEADS, 1), (NSA_HEADS, 128))
    nsa_slopes_g = jnp.concatenate([nsa_slopes8.reshape(2, 4, 128), jnp.zeros((2, 4, 128), F32)], axis=1)
    swa_slopes16 = jnp.broadcast_to(jnp.asarray(SWA_SLOPES, F32).reshape(SWA_HEADS, 1), (SWA_HEADS, 128))

    fox_pages = _t_last_view(cache_fox_kv[0]).reshape(-1, 2, 256, PAGE_SIZE)
    lf_pages = _t_last_view(cache_fox_logf[0])
    cmp_pages = _t_last_view(cache_nsa_cmp_kv[0]).reshape(-1, 256, PAGE_SIZE)
    sel_pages = _t_last_view(cache_nsa_sel_kv[0])
    win_state = _t_last_view(state_nsa_win[0])
    swa_state = _t_last_view(state_swa[0])

    (q_p, sm_p, fkvT, cmpT, selT, winT, logfT, fkvTb, selTb, winTb, kselA, kwinA) = _even_proj(
        x_prompt, g_mix[0], e_wn, e_wt, bsm, bcol)
    c, kfoxA = _cumsum(logfT, sm_p, fkvTb)
    o_fox = _fox_attn(q_p, c, kfoxA, fkvTb)
    kcT, kc = _cmp_prompt(cmpT, pe_rows.T, w_augT)
    o_nsa = _nsa_attn(q_p, sm_p, kcT, kc, kselA, selTb, kwinA, winTb)
    xp = _out_ffn(x_prompt.reshape(b * t, d), [o_fox.reshape(b * t, 512), o_nsa.reshape(b * t, 512)],
                  e_wo, g_ffn[0], wg[0], wu[0], wd[0])

    xs0 = x_sample.reshape(1, nb, d)
    (q_s, sm_s, fkvT_s, cmpT_s, selT_s, winT_s, logfT_s, _, _, _, _, _, rows_s) = _even_proj(
        xs0, g_mix[0], e_wn, e_wt, bsm, bcol, e_wr)
    q_s = q_s[0].reshape(nb, FOX_HEADS + NSA_HEADS, 128)[:, :, :HEAD_DIM].reshape(nb, 1024)
    sm_s, rows_s = sm_s[0], rows_s[0]
    rows_sb = rows_s.astype(BF16).astype(F32)
    qbd_f = _block_diag_q(q_s[:, :512], FOX_KV_HEADS, 2)
    lfn = jnp.broadcast_to(sm_s[:, :FOX_HEADS, None], (nb, FOX_HEADS, 128))
    o_fox_s = _fox_decode(page_table, qbd_f, lfn, rows_sb[:, None, 0:256], rows_sb[:, None, 256:512],
                          fox_pages, lf_pages)
    o_fox_s = _block_diag_take(o_fox_s, FOX_KV_HEADS, 2)
    kc_s = _cmp_decode(page_table, cmp_pages, pe_rows, wbd_b)
    qbd_n = _block_diag_q(q_s[:, 512:], NSA_KV_HEADS, 4)
    ocmp_s, idx_s = _nsa_select(qbd_n, kc_s, nsa_slopes8, past)
    ocmp_s = _block_diag_take(ocmp_s, NSA_KV_HEADS, 4).reshape(nb, NSA_KV_HEADS, 4, HEAD_DIM)
    pad4 = jnp.zeros((nb, NSA_KV_HEADS, 4, HEAD_DIM), F32)
    ocmp_s = jnp.concatenate([ocmp_s, pad4], axis=2)
    qn_s = jnp.concatenate([q_s[:, 512:].reshape(nb, NSA_KV_HEADS, 4, HEAD_DIM), pad4.astype(BF16)], axis=2)
    nkv_s = rows_s[:, 512:].reshape(nb, 6, NSA_KV_HEADS, HEAD_DIM)
    new_s = jnp.transpose(nkv_s[:, 2:6], (0, 2, 1, 3))
    gate_s = sm_s[:, FOX_HEADS:FOX_HEADS + 24].reshape(nb, NSA_KV_HEADS, 4, 3)
    gate_s = jnp.zeros((nb, NSA_KV_HEADS, 8, 128), F32).at[:, :, :4, :3].set(gate_s)
    o_nsa_s, win_new = _nsa_decode(page_table, idx_s, qn_s, new_s, gate_s, ocmp_s, nsa_slopes_g, win_state,
                                   sel_pages, past)
    o_nsa_s = o_nsa_s[:, :, :4].reshape(nb, 512)
    xs = _out_ffn(x_sample.reshape(nb, d), [o_fox_s.astype(BF16), o_nsa_s.astype(BF16)],
                  e_wo, g_ffn[0], wg[0], wu[0], wd[0])

    q1, kvT1, kvT1b = _odd_proj(xp.reshape(b, t, d), g_mix[1], o_wn, o_wt)
    o_swa = _swa_attn(q1, kvT1b, sinks_row)
    y_prompt = _out_ffn(xp, [o_swa.reshape(b * t, 1024)], o_wo, g_ffn[1], wg[1], wu[1], wd[1], g_fin)

    q1s, kvT1s, _, rows1s = _odd_proj(xs.reshape(1, nb, d), g_mix[1], o_wn, o_wt, o_wr)
    qbd_s = _block_diag_q(q1s[0], SWA_KV_HEADS, 8)
    o_swa_s, swa_new = _swa_decode(qbd_s, rows1s[0][:, None, :], sinks_col, swa_slopes16, swa_state)
    o_swa_s = _block_diag_take(o_swa_s, SWA_KV_HEADS, 8)
    y_sample = _out_ffn(xs, [o_swa_s.astype(BF16)], o_wo, g_ffn[1], wg[1], wu[1], wd[1], g_fin)

    def rows_first(xT, *feat):
        n, _, rows = xT.shape
        return _t_first_view(xT.reshape((n,) + feat + (rows,)))[None]

    n_win = min(NSA_WINDOW, t)
    n_swa = min(SWA_WINDOW, t)
    return (y_prompt.reshape(b, t, d), y_sample.reshape(nb, 1, d),
            rows_first(fkvT, 2, FOX_KV_HEADS, HEAD_DIM),
            jnp.transpose(rows_first(fkvT_s, 2, FOX_KV_HEADS, HEAD_DIM), (0, 2, 1, 3, 4, 5)),
            rows_first(logfT, FOX_HEADS),
            jnp.transpose(rows_first(logfT_s, FOX_HEADS), (0, 2, 1, 3)),
            rows_first(cmpT, 2, NSA_KV_HEADS, HEAD_DIM),
            jnp.transpose(rows_first(cmpT_s, 2, NSA_KV_HEADS, HEAD_DIM), (0, 2, 1, 3, 4, 5)),
            rows_first(selT, 2, NSA_KV_HEADS, HEAD_DIM),
            jnp.transpose(rows_first(selT_s, 2, NSA_KV_HEADS, HEAD_DIM), (0, 2, 1, 3, 4, 5)),
            rows_first(winT[:, :, t - n_win:], 2, NSA_KV_HEADS, HEAD_DIM),
            _t_first_view(win_new)[None],
            rows_first(kvT1[:, :, t - n_swa:], 2, SWA_KV_HEADS, HEAD_DIM),
            _t_first_view(swa_new)[None])
```

```python
import functools

import numpy as np
import jax
import jax.numpy as jnp
from jax import lax
from jax.experimental import pallas as pl
from jax.experimental.pallas import tpu as pltpu

F32 = jnp.float32
BF16 = jnp.bfloat16

HEAD_DIM = 64
PAGE_SIZE = 128
FOX_HEADS = 8
FOX_KV_HEADS = 4
NSA_HEADS = 8
NSA_KV_HEADS = 2
NSA_CMP_BLOCK = 32
NSA_SEL_BLOCK = 64
NSA_TOPK = 16
NSA_WINDOW = 512
SWA_HEADS = 16
SWA_KV_HEADS = 2
SWA_WINDOW = 128
ATTN_SCALE = HEAD_DIM ** -0.5
RMS_EPS = 1e-6
NEG_INF = -1e30
TINY = 1e-30
FORCED_SCORE = 1e6
FOX_TQ = 1024
NSA_TQ = 256

VMEM_LIMIT_BYTES = 56 * 1024 * 1024


def _alibi_slopes(n_heads):
    exps = np.arange(1, n_heads + 1, dtype=np.float32)
    return np.power(np.float32(2.0), -8.0 * exps / n_heads).astype(np.float32)


NSA_SLOPES = [float(v) for v in _alibi_slopes(NSA_HEADS)]
SWA_SLOPES = [float(v) for v in _alibi_slopes(SWA_HEADS)]


def _cparams(*sem):
    return pltpu.CompilerParams(dimension_semantics=sem, vmem_limit_bytes=VMEM_LIMIT_BYTES)


def _dot(a, b):
    return jnp.dot(a, b, preferred_element_type=F32)


def _dot_nt(a, b):
    return lax.dot_general(a, b, (((1,), (1,)), ((), ())), preferred_element_type=F32)


def _split3(x):
    hi = x.astype(BF16)
    r1 = x - hi.astype(F32)
    mid = r1.astype(BF16)
    lo = (r1 - mid.astype(F32)).astype(BF16)
    return hi, mid, lo


def _rms(x, g):
    return x * lax.rsqrt(jnp.mean(x * x, axis=-1, keepdims=True) + RMS_EPS) * g


def _log_sigmoid(x):
    return jnp.minimum(x, 0.0) - jnp.log1p(jnp.exp(-jnp.abs(x)))


def _iota(shape, dim):
    return lax.broadcasted_iota(jnp.int32, shape, dim)


def _lane_pick(x, lane_idx):
    lane = _iota(x.shape, 1)
    return jnp.sum(jnp.where(lane == lane_idx, x, 0.0), axis=-1, keepdims=True)


def _row_to_col(row):
    n = row.shape[-1]
    full = jnp.broadcast_to(row, (n, n))
    return jnp.sum(jnp.where(_iota((n, n), 0) == _iota((n, n), 1), full, 0.0), axis=-1, keepdims=True)


def _pos_features(pos):
    row = _iota((8, pos.shape[-1]), 0)
    hi = jnp.left_shift(jnp.right_shift(pos, 7), 7).astype(F32)
    lo = jnp.bitwise_and(pos, 127).astype(F32)
    return jnp.where(row < 2, 1.0, jnp.where(row == 2, hi, jnp.where(row == 3, lo, 0.0)))


def _alibi_query(qh, tpos, slope):
    lane = _iota(qh.shape, 1)
    hi = jnp.left_shift(jnp.right_shift(tpos, 7), 7).astype(F32)
    lo = jnp.bitwise_and(tpos, 127).astype(F32)
    a_hi = (-slope * hi).astype(BF16)
    a_lo = (-slope * lo).astype(BF16)
    m = (jnp.zeros_like(hi) + slope).astype(BF16)
    return jnp.where(lane == 64, a_hi, jnp.where(lane == 65, a_lo, jnp.where((lane == 66) | (lane == 67), m, qh)))


def _even_proj_kernel(*refs, with_rows):
    if with_rows:
        (x_ref, g_ref, wn_ref, wt_ref, bsm_ref, bcol_ref, wr_ref,
         q_ref, sm_ref, fkvT_ref, cmpT_ref, selT_ref, winT_ref, logfT_ref,
         fkvTb_ref, selTb_ref, winTb_ref, kselA_ref, kwinA_ref, rows_ref) = refs
    else:
        (x_ref, g_ref, wn_ref, wt_ref, bsm_ref, bcol_ref,
         q_ref, sm_ref, fkvT_ref, cmpT_ref, selT_ref, winT_ref, logfT_ref,
         fkvTb_ref, selTb_ref, winTb_ref, kselA_ref, kwinA_ref) = refs
    tm = x_ref.shape[1]
    h = _rms(x_ref[0], g_ref[...]).astype(BF16)
    n = _dot(h, wn_ref[...])
    q_ref[0] = n[:, :2048].astype(BF16)
    sm = n[:, 2048:] + bsm_ref[...]
    lane = _iota(sm.shape, 1)
    sm_ref[0] = jnp.where(lane < FOX_HEADS, _log_sigmoid(sm), jax.nn.sigmoid(sm))
    t = _dot_nt(wt_ref[...], h)
    fkv = t[0:512]
    sel = t[768:1024]
    win = t[1024:1280]
    fkvT_ref[0] = fkv
    cmpT_ref[0] = t[512:768]
    selT_ref[0] = sel
    winT_ref[0] = win
    logfT_ref[0] = _log_sigmoid(t[1280:1288] + bcol_ref[...])
    fkvTb_ref[0] = fkv.astype(BF16)
    selTb_ref[0] = sel.astype(BF16)
    winTb_ref[0] = win.astype(BF16)
    ns = kselA_ref.shape[2] - 128
    pos = pl.program_id(1) * tm + _iota((1, tm), 1)
    feat = _pos_features(pos)
    pad = jnp.zeros((56, tm), F32)
    onehot = (_iota((ns, tm), 0) == jnp.right_shift(pos, 6)).astype(F32)
    for g in range(NSA_KV_HEADS):
        ksel = sel[g * HEAD_DIM:(g + 1) * HEAD_DIM]
        kwin = win[g * HEAD_DIM:(g + 1) * HEAD_DIM]
        kselA_ref[0, g] = jnp.concatenate([ksel, feat, pad, onehot], axis=0).astype(BF16)
        kwinA_ref[0, g] = jnp.concatenate([kwin, feat, pad], axis=0).astype(BF16)
    if with_rows:
        rows_ref[0] = _dot(h, wr_ref[...])


def _even_proj(x, g, wn, wt, bsm, bcol, wr=None):
    bx, tx, d = x.shape
    tm = min(512, tx)
    ns = max(tx // NSA_SEL_BLOCK, 8)
    with_rows = wr is not None
    const2 = lambda b, j: (0, 0)
    in_specs = [pl.BlockSpec((1, tm, d), lambda b, j: (b, j, 0)),
                pl.BlockSpec((1, d), const2),
                pl.BlockSpec(wn.shape, const2),
                pl.BlockSpec(wt.shape, const2),
                pl.BlockSpec((1, 128), const2),
                pl.BlockSpec((FOX_HEADS, 1), const2)]
    args = [x, g, wn, wt, bsm, bcol]
    tspec = lambda rows: pl.BlockSpec((1, rows, tm), lambda b, j: (b, 0, j))
    aspec = lambda rows: pl.BlockSpec((1, NSA_KV_HEADS, rows, tm), lambda b, j: (b, 0, 0, j))
    out_specs = [pl.BlockSpec((1, tm, 2048), lambda b, j: (b, j, 0)),
                 pl.BlockSpec((1, tm, 128), lambda b, j: (b, j, 0)),
                 tspec(512), tspec(256), tspec(256), tspec(256), tspec(FOX_HEADS),
                 tspec(512), tspec(256), tspec(256), aspec(128 + ns), aspec(128)]
    out_shape = [jax.ShapeDtypeStruct((bx, tx, 2048), BF16),
                 jax.ShapeDtypeStruct((bx, tx, 128), F32),
                 jax.ShapeDtypeStruct((bx, 512, tx), F32),
                 jax.ShapeDtypeStruct((bx, 256, tx), F32),
                 jax.ShapeDtypeStruct((bx, 256, tx), F32),
                 jax.ShapeDtypeStruct((bx, 256, tx), F32),
                 jax.ShapeDtypeStruct((bx, FOX_HEADS, tx), F32),
                 jax.ShapeDtypeStruct((bx, 512, tx), BF16),
                 jax.ShapeDtypeStruct((bx, 256, tx), BF16),
                 jax.ShapeDtypeStruct((bx, 256, tx), BF16),
                 jax.ShapeDtypeStruct((bx, NSA_KV_HEADS, 128 + ns, tx), BF16),
                 jax.ShapeDtypeStruct((bx, NSA_KV_HEADS, 128, tx), BF16)]
    if with_rows:
        in_specs.append(pl.BlockSpec(wr.shape, const2))
        args.append(wr)
        out_specs.append(pl.BlockSpec((1, tm, wr.shape[1]), lambda b, j: (b, j, 0)))
        out_shape.append(jax.ShapeDtypeStruct((bx, tx, wr.shape[1]), F32))
    return pl.pallas_call(
        functools.partial(_even_proj_kernel, with_rows=with_rows),
        grid=(bx, tx // tm), in_specs=in_specs, out_specs=out_specs, out_shape=out_shape,
        compiler_params=_cparams("parallel", "parallel"), name="even_proj")(*args)


def _odd_proj_kernel(*refs, with_rows):
    if with_rows:
        x_ref, g_ref, wn_ref, wt_ref, wr_ref, q_ref, kvT_ref, kvTb_ref, rows_ref = refs
    else:
        x_ref, g_ref, wn_ref, wt_ref, q_ref, kvT_ref, kvTb_ref = refs
    h = _rms(x_ref[0], g_ref[...]).astype(BF16)
    q_ref[0] = _dot(h, wn_ref[...]).astype(BF16)
    t = _dot_nt(wt_ref[...], h)
    kvT_ref[0] = t
    kvTb_ref[0] = t.astype(BF16)
    if with_rows:
        rows_ref[0] = _dot(h, wr_ref[...])


def _odd_proj(x, g, wn, wt, wr=None):
    bx, tx, d = x.shape
    tm = min(512, tx)
    with_rows = wr is not None
    const2 = lambda b, j: (0, 0)
    in_specs = [pl.BlockSpec((1, tm, d), lambda b, j: (b, j, 0)),
                pl.BlockSpec((1, d), const2),
                pl.BlockSpec(wn.shape, const2),
                pl.BlockSpec(wt.shape, const2)]
    args = [x, g, wn, wt]
    out_specs = [pl.BlockSpec((1, tm, 1024), lambda b, j: (b, j, 0)),
                 pl.BlockSpec((1, 256, tm), lambda b, j: (b, 0, j)),
                 pl.BlockSpec((1, 256, tm), lambda b, j: (b, 0, j))]
    out_shape = [jax.ShapeDtypeStruct((bx, tx, 1024), BF16),
                 jax.ShapeDtypeStruct((bx, 256, tx), F32),
                 jax.ShapeDtypeStruct((bx, 256, tx), BF16)]
    if with_rows:
        in_specs.append(pl.BlockSpec(wr.shape, const2))
        args.append(wr)
        out_specs.append(pl.BlockSpec((1, tm, 256), lambda b, j: (b, j, 0)))
        out_shape.append(jax.ShapeDtypeStruct((bx, tx, 256), F32))
    return pl.pallas_call(
        functools.partial(_odd_proj_kernel, with_rows=with_rows),
        grid=(bx, tx // tm), in_specs=in_specs, out_specs=out_specs, out_shape=out_shape,
        compiler_params=_cparams("parallel", "parallel"), name="odd_proj")(*args)


def _cumsum_kernel(lT_ref, sm_ref, kTb_ref, triu_ref, tril_ref, c_ref, kA_ref, carT_ref, carR_ref):
    @pl.when(pl.program_id(1) == 0)
    def _():
        carT_ref[...] = jnp.zeros_like(carT_ref)
        carR_ref[...] = jnp.zeros_like(carR_ref)

    tm = lT_ref.shape[-1]
    triu = triu_ref[...]
    hi, mid, lo = _split3(lT_ref[0])
    cT = _dot(hi, triu) + _dot(mid, triu) + _dot(lo, triu) + carT_ref[:, 0:1]
    carT_ref[...] = jnp.broadcast_to(cT[:, tm - 1:tm], carT_ref.shape)
    tril = tril_ref[...]
    hi, mid, lo = _split3(sm_ref[0])
    c = _dot(tril, hi) + _dot(tril, mid) + _dot(tril, lo) + carR_ref[0:1, :]
    c_ref[0] = c
    carR_ref[...] = jnp.broadcast_to(c[tm - 1:tm, :], carR_ref.shape)

    row = _iota((8, tm), 0)
    ones8 = jnp.where(row < 3, 1.0, 0.0)
    pad = jnp.zeros((40, tm), F32)

    def neg_pieces(v):
        hi, mid, lo = _split3(v)
        return jnp.where(row == 0, -hi.astype(F32), jnp.where(row == 1, -mid.astype(F32),
                                                              jnp.where(row == 2, -lo.astype(F32), 0.0)))

    kT = kTb_ref[0].astype(F32)
    for g in range(FOX_KV_HEADS):
        kA_ref[0, g] = jnp.concatenate(
            [kT[g * HEAD_DIM:(g + 1) * HEAD_DIM], ones8, neg_pieces(cT[2 * g:2 * g + 1]),
             neg_pieces(cT[2 * g + 1:2 * g + 2]), pad], axis=0).astype(BF16)


def _cumsum(logfT, sm, fkvTb):
    b, _, t = logfT.shape
    tm = min(512, t)
    i = np.arange(tm)
    triu = jnp.asarray(i[:, None] <= i[None, :], BF16)
    tril = jnp.asarray(i[:, None] >= i[None, :], BF16)
    return pl.pallas_call(
        _cumsum_kernel, grid=(b, t // tm),
        in_specs=[pl.BlockSpec((1, FOX_HEADS, tm), lambda b, j: (b, 0, j)),
                  pl.BlockSpec((1, tm, 128), lambda b, j: (b, j, 0)),
                  pl.BlockSpec((1, 256, tm), lambda b, j: (b, 0, j)),
                  pl.BlockSpec((tm, tm), lambda b, j: (0, 0)),
                  pl.BlockSpec((tm, tm), lambda b, j: (0, 0))],
        out_specs=[pl.BlockSpec((1, tm, 128), lambda b, j: (b, j, 0)),
                   pl.BlockSpec((1, FOX_KV_HEADS, 128, tm), lambda b, j: (b, 0, 0, j))],
        out_shape=[jax.ShapeDtypeStruct((b, t, 128), F32),
                   jax.ShapeDtypeStruct((b, FOX_KV_HEADS, 128, t), BF16)],
        scratch_shapes=[pltpu.VMEM((FOX_HEADS, 128), F32), pltpu.VMEM((8, 128), F32)],
        compiler_params=_cparams("parallel", "arbitrary"), name="fox_cumsum")(logfT, sm, fkvTb, triu, tril)


def _softmax_step(carry, s, vT, mask=None):
    m, l, acc = carry
    m_new = jnp.maximum(m, jnp.max(s, axis=-1, keepdims=True))
    p = jnp.exp(s - m_new)
    if mask is not None:
        p = jnp.where(mask, p, 0.0)
    alpha = jnp.exp(m - m_new)
    l = alpha * l + jnp.sum(p, axis=-1, keepdims=True)
    acc = alpha * acc + _dot_nt(p.astype(BF16), vT)
    return m_new, l, acc


def _softmax_first(s, vT, mask):
    m = jnp.max(s, axis=-1, keepdims=True)
    p = jnp.where(mask, jnp.exp(s - m), 0.0)
    return m, jnp.sum(p, axis=-1, keepdims=True), _dot_nt(p.astype(BF16), vT)


def _fox_attn_kernel(q_ref, c_ref, kA_ref, vT_ref, o_ref):
    g = pl.program_id(1)
    iq = pl.program_id(2)
    tq = q_ref.shape[1]
    q2 = q_ref[0]
    cblk = c_ref[0]
    lane = _iota((tq, 128), 1)
    one = jnp.ones((tq, 1), BF16)
    qa = []
    for r in range(2):
        hi, mid, lo = _split3(_lane_pick(cblk, 2 * g + r))
        base = 72 + 8 * r
        qa.append(jnp.where(lane == 64, hi, jnp.where(lane == 65, mid, jnp.where(
            lane == 66, lo, jnp.where((lane >= base) & (lane < base + 3), one, q2[:, r * 128:(r + 1) * 128])))))
    causal = _iota((tq, tq), 1) <= _iota((tq, tq), 0)

    def chunk(j):
        off = pl.multiple_of(j * tq, tq)
        return kA_ref[0, 0, :, pl.ds(off, tq)], vT_ref[0, :, pl.ds(off, tq)]

    kA, vT = chunk(iq)
    carry = tuple(_softmax_first(jnp.where(causal, _dot(qa[r], kA), NEG_INF), vT, causal) for r in range(2))

    def body(j, carry):
        kA, vT = chunk(j)
        return tuple(_softmax_step(carry[r], _dot(qa[r], kA), vT) for r in range(2))

    carry = lax.fori_loop(0, iq, body, carry)
    o_ref[0] = jnp.concatenate([acc / jnp.maximum(l, TINY) for (_, l, acc) in carry], axis=-1).astype(BF16)


def _fox_attn(q, c, kA, fkvTb):
    b, t, _ = q.shape
    tq = min(FOX_TQ, t)
    return pl.pallas_call(
        _fox_attn_kernel, grid=(b, FOX_KV_HEADS, t // tq),
        in_specs=[pl.BlockSpec((1, tq, 256), lambda b, g, i: (b, i, g)),
                  pl.BlockSpec((1, tq, 128), lambda b, g, i: (b, i, 0)),
                  pl.BlockSpec((1, 1, 128, t), lambda b, g, i: (b, g, 0, 0)),
                  pl.BlockSpec((1, HEAD_DIM, t), lambda b, g, i: (b, FOX_KV_HEADS + g, 0))],
        out_specs=pl.BlockSpec((1, tq, 128), lambda b, g, i: (b, i, g)),
        out_shape=jax.ShapeDtypeStruct((b, t, 512), BF16),
        compiler_params=_cparams("parallel", "parallel", "arbitrary"), name="fox_attn")(q, c, kA, fkvTb)


def _cmp_prompt_kernel(x_ref, pool_ref, peT_ref, wT_ref, featT_ref, kcT_ref, kc_ref):
    x = x_ref[0]
    hi = x.astype(BF16)
    lo = (x - hi.astype(F32)).astype(BF16)
    pool = pool_ref[...]
    meanT = _dot(hi, pool) + _dot(lo, pool)
    meanT = meanT + jnp.mean(peT_ref[...], axis=-1, keepdims=True)
    kcT = _dot(wT_ref[...], meanT.astype(BF16)) + featT_ref[...]
    kcT_ref[0] = kcT.astype(BF16)
    kc_ref[0] = kcT.T.astype(BF16)


def _cmp_columns(n_blocks):
    j = np.arange(n_blocks)
    return np.where(j % 2 == 0, j // 2, n_blocks // 2 + j // 2)


def _cmp_prompt(cmpT, peT, w_augT):
    b, _, t = cmpT.shape
    nc = t // NSA_CMP_BLOCK
    col_of_block = _cmp_columns(nc)
    cols = col_of_block[np.arange(t) // NSA_CMP_BLOCK]
    pool = jnp.asarray((cols[:, None] == np.arange(nc)[None, :]) / NSA_CMP_BLOCK, BF16)
    cpos = np.zeros(nc, np.int64)
    cpos[col_of_block] = NSA_CMP_BLOCK * (np.arange(nc) + 1) - 1
    featT = np.zeros((512, nc), np.float32)
    for g in range(NSA_KV_HEADS):
        featT[g * 128 + 64] = 1.0
        featT[g * 128 + 65] = 1.0
        featT[g * 128 + 66] = (cpos >> 7) << 7
        featT[g * 128 + 67] = cpos & 127
    return pl.pallas_call(
        _cmp_prompt_kernel, grid=(b,),
        in_specs=[pl.BlockSpec((1, 256, t), lambda b: (b, 0, 0)),
                  pl.BlockSpec((t, nc), lambda b: (0, 0)),
                  pl.BlockSpec((256, NSA_CMP_BLOCK), lambda b: (0, 0)),
                  pl.BlockSpec((512, 256), lambda b: (0, 0)),
                  pl.BlockSpec((512, nc), lambda b: (0, 0))],
        out_specs=[pl.BlockSpec((1, 512, nc), lambda b: (b, 0, 0)),
                   pl.BlockSpec((1, nc, 512), lambda b: (b, 0, 0))],
        out_shape=[jax.ShapeDtypeStruct((b, 512, nc), BF16), jax.ShapeDtypeStruct((b, nc, 512), BF16)],
        compiler_params=_cparams("parallel"), name="nsa_compress")(cmpT, pool, peT, w_augT, jnp.asarray(featT))


def _topk_select(val, idx_f, n, n_top, axis):
    sel = jnp.zeros_like(val)
    for _ in range(n_top):
        mx = jnp.max(val, axis=axis, keepdims=True)
        idx = jnp.min(jnp.where(val == mx, idx_f, float(n)), axis=axis, keepdims=True)
        hit = idx_f == idx
        sel = jnp.where(hit, 1.0, sel)
        val = jnp.where(hit, -2.0, val)
    return sel


def _nsa_attn_kernel(q_ref, sm_ref, kcT_ref, vcT_ref, kc_ref, ks_ref, vs_ref, kw_ref, vw_ref, eye_ref, o_ref,
                     live_ref, *, tk, nwin):
    g = pl.program_id(1)
    iq = pl.program_id(2)
    tq = q_ref.shape[1]
    nc = kcT_ref.shape[-1]
    ns = nc // 2
    s0 = iq * tq
    tpos = s0 + _iota((tq, 1), 0)
    tpos4 = s0 + jnp.bitwise_and(_iota((4 * tq, 1), 0), tq - 1)
    tposT = s0 + _iota((1, tq), 1)
    q4 = q_ref[0]
    gates = sm_ref[0]
    qs = jnp.concatenate(
        [_alibi_query(q4[:, r * 128:(r + 1) * 128], tpos, jnp.where(g == 0, NSA_SLOPES[r], NSA_SLOPES[4 + r]))
         for r in range(4)], axis=0)

    def cmp_pos(col):
        j = jnp.where(col < ns, 2 * col, 2 * (col - ns) + 1)
        return NSA_CMP_BLOCK * (j + 1) - 1

    mask_c = tpos4 >= cmp_pos(_iota((1, nc), 1))
    s = jnp.where(mask_c, _dot(qs, kcT_ref[0]), NEG_INF)
    m = jnp.max(s, axis=-1, keepdims=True)
    p = jnp.where(mask_c, jnp.exp(s - m), 0.0)
    p = p / jnp.maximum(jnp.sum(p, axis=-1, keepdims=True), TINY)
    o_cmp = _dot_nt(p.astype(BF16), vcT_ref[0, :HEAD_DIM, :])

    sT = _dot_nt(kc_ref[0], qs)
    maskT = tposT >= cmp_pos(_iota((nc, 1), 0))
    imp = None
    for r in range(4):
        sr = jnp.where(maskT, sT[:, r * tq:(r + 1) * tq], NEG_INF)
        mr = jnp.max(sr, axis=0, keepdims=True)
        pr = jnp.where(maskT, jnp.exp(sr - mr), 0.0)
        pr = pr / jnp.maximum(jnp.sum(pr, axis=0, keepdims=True), TINY)
        imp = pr if imp is None else imp + pr
    imp = imp[:ns] + imp[ns:]
    blk = _iota((ns, 1), 0)
    cur = jnp.right_shift(tposT, 6)
    visible = blk * NSA_SEL_BLOCK <= tposT
    forced = (blk == 0) | (blk == cur) | (blk == cur - 1)
    val = jnp.where(visible, jnp.where(forced, FORCED_SCORE, imp), -1.0)
    sel = _topk_select(val, blk.astype(F32), ns, min(NSA_TOPK, ns), 0)
    selbT = jnp.where((sel > 0.0) & visible, 0.0, NEG_INF).astype(BF16)
    selb = _dot_nt(eye_ref[...], selbT).astype(BF16)
    q2 = jnp.concatenate([qs, jnp.concatenate([selb] * 4, axis=0)], axis=1)

    def sel_chunk(j):
        off = pl.multiple_of(j * tk, tk)
        return ks_ref[0, 0, :, pl.ds(off, tk)], vs_ref[0, :, pl.ds(off, tk)], off

    n_full = s0 // tk
    kA, vT, off = sel_chunk(n_full)
    ok = (off + _iota((1, tk), 1)) <= tpos4
    carry = _softmax_first(jnp.where(ok, _dot(q2, kA), NEG_INF), vT, ok)

    picked = jnp.max(jnp.where((sel > 0.0) & visible, 1.0, 0.0), axis=-1, keepdims=True)
    per_chunk = tk // NSA_SEL_BLOCK
    n_live = jnp.int32(0)
    for j in range(ns // per_chunk):
        live = (jnp.max(picked[j * per_chunk:(j + 1) * per_chunk]) > 0.0) & (j < n_full)
        live_ref[n_live] = j
        n_live = n_live + live.astype(jnp.int32)

    def body(i, carry):
        kA, vT, _ = sel_chunk(live_ref[i])
        return _softmax_step(carry, _dot(q2, kA), vT)

    _, l, acc = lax.fori_loop(0, n_live, body, carry)
    o_sel = acc / jnp.maximum(l, TINY)

    start = pl.multiple_of(jnp.maximum(s0 - NSA_WINDOW, 0), 128)
    dist = tpos4 - (start + _iota((1, nwin), 1))
    ok = (dist >= 0) & (dist <= NSA_WINDOW)
    s = jnp.where(ok, _dot(qs, kw_ref[0, 0, :, pl.ds(start, nwin)]), NEG_INF)
    _, l, acc = _softmax_first(s, vw_ref[0, :, pl.ds(start, nwin)], ok)
    o_win = acc / jnp.maximum(l, TINY)

    outs = []
    for r in range(4):
        rows = slice(r * tq, (r + 1) * tq)
        base = FOX_HEADS + (g * 4 + r) * 3
        outs.append(_lane_pick(gates, base) * o_cmp[rows] + _lane_pick(gates, base + 1) * o_sel[rows]
                    + _lane_pick(gates, base + 2) * o_win[rows])
    o_ref[0] = jnp.concatenate(outs, axis=-1).astype(BF16)


def _nsa_attn(q, sm, kcT, kc, kselA, selTb, kwinA, winTb):
    b, t, _ = q.shape
    tq = min(NSA_TQ, t)
    tk = min(512, t)
    nwin = min(NSA_WINDOW + tq, t)
    nc = kcT.shape[-1]
    ka = kselA.shape[2]
    eye = jnp.asarray(np.eye(tq), BF16)
    return pl.pallas_call(
        functools.partial(_nsa_attn_kernel, tk=tk, nwin=nwin), grid=(b, NSA_KV_HEADS, t // tq),
        in_specs=[pl.BlockSpec((1, tq, 512), lambda b, g, i: (b, i, 2 + g)),
                  pl.BlockSpec((1, tq, 128), lambda b, g, i: (b, i, 0)),
                  pl.BlockSpec((1, 128, nc), lambda b, g, i: (b, g, 0)),
                  pl.BlockSpec((1, 128, nc), lambda b, g, i: (b, 2 + g, 0)),
                  pl.BlockSpec((1, nc, 128), lambda b, g, i: (b, 0, g)),
                  pl.BlockSpec((1, 1, ka, t), lambda b, g, i: (b, g, 0, 0)),
                  pl.BlockSpec((1, HEAD_DIM, t), lambda b, g, i: (b, 2 + g, 0)),
                  pl.BlockSpec((1, 1, 128, t), lambda b, g, i: (b, g, 0, 0)),
                  pl.BlockSpec((1, HEAD_DIM, t), lambda b, g, i: (b, 2 + g, 0)),
                  pl.BlockSpec((tq, tq), lambda b, g, i: (0, 0))],
        out_specs=pl.BlockSpec((1, tq, 256), lambda b, g, i: (b, i, g)),
        out_shape=jax.ShapeDtypeStruct((b, t, 512), BF16),
        scratch_shapes=[pltpu.SMEM((t // tk + 1,), jnp.int32)],
        compiler_params=_cparams("parallel", "parallel", "arbitrary"), name="nsa_attn")(
            q, sm, kcT, kcT, kc, kselA, selTb, kwinA, winTb, eye)


def _swa_attn_kernel(q_ref, kT_ref, vT_ref, sink_ref, o_ref, *, nkeys):
    g = pl.program_id(1)
    iq = pl.program_id(2)
    tq = q_ref.shape[1]
    s0 = iq * tq
    tpos = s0 + _iota((tq, 1), 0)
    start = pl.multiple_of(jnp.maximum(s0 - SWA_WINDOW, 0), 128)
    kT = kT_ref[0, :, pl.ds(start, nkeys)]
    vT = vT_ref[0, :, pl.ds(start, nkeys)]
    dist = tpos - (start + _iota((1, nkeys), 1))
    ok = (dist >= 0) & (dist <= SWA_WINDOW)
    distf = dist.astype(F32)
    q8 = q_ref[0]
    sinks = sink_ref[...]
    outs = []
    for r in range(8):
        slope = jnp.where(g == 0, SWA_SLOPES[r], SWA_SLOPES[8 + r])
        sink = _lane_pick(sinks, g * 8 + r)
        s = jnp.where(ok, _dot(q8[:, r * HEAD_DIM:(r + 1) * HEAD_DIM], kT) - slope * distf, NEG_INF)
        m = jnp.maximum(jnp.max(s, axis=-1, keepdims=True), sink)
        p = jnp.where(ok, jnp.exp(s - m), 0.0)
        den = jnp.sum(p, axis=-1, keepdims=True) + jnp.exp(sink - m)
        outs.append(_dot_nt(p.astype(BF16), vT) / jnp.maximum(den, TINY))
    o_ref[0] = jnp.concatenate(outs, axis=-1).astype(BF16)


def _swa_attn(q, kvTb, sinks):
    b, t, _ = q.shape
    tq = 128
    nkeys = min(SWA_WINDOW + tq, t)
    return pl.pallas_call(
        functools.partial(_swa_attn_kernel, nkeys=nkeys), grid=(b, SWA_KV_HEADS, t // tq),
        in_specs=[pl.BlockSpec((1, tq, 512), lambda b, g, i: (b, i, g)),
                  pl.BlockSpec((1, HEAD_DIM, t), lambda b, g, i: (b, g, 0)),
                  pl.BlockSpec((1, HEAD_DIM, t), lambda b, g, i: (b, 2 + g, 0)),
                  pl.BlockSpec((1, 128), lambda b, g, i: (0, 0))],
        out_specs=pl.BlockSpec((1, tq, 512), lambda b, g, i: (b, i, g)),
        out_shape=jax.ShapeDtypeStruct((b, t, 1024), BF16),
        compiler_params=_cparams("parallel", "parallel", "arbitrary"), name="swa_attn")(q, kvTb, kvTb, sinks)


def _out_ffn_kernel(*refs, n_mix, final):
    x_ref = refs[0]
    o_refs = refs[1:1 + n_mix]
    wo_ref, gf_ref, wg_ref, wu_ref, wd_ref = refs[1 + n_mix:6 + n_mix]
    gfin_ref = refs[6 + n_mix] if final else None
    y_ref = refs[-1]
    mix = [o_ref[...] for o_ref in o_refs]
    mix = mix[0] if n_mix == 1 else jnp.concatenate(mix, axis=-1)
    x = x_ref[...] + _dot(mix, wo_ref[...])
    h = _rms(x, gf_ref[...]).astype(BF16)
    gt = _dot(h, wg_ref[...])
    up = _dot(h, wu_ref[...])
    x = x + _dot((jax.nn.silu(gt) * up).astype(BF16), wd_ref[...])
    if final:
        x = _rms(x, gfin_ref[...])
    y_ref[...] = x


def _out_ffn(x, mixes, wo, gf, wg, wu, wd, gfin=None):
    rows, d = x.shape
    tm = min(512, rows)
    final = gfin is not None
    const = lambda i: (0, 0)
    resident = lambda shape: pl.BlockSpec(shape, const, pipeline_mode=pl.Buffered(1))
    in_specs = [pl.BlockSpec((tm, d), lambda i: (i, 0))]
    in_specs += [pl.BlockSpec((tm, m.shape[1]), lambda i: (i, 0)) for m in mixes]
    in_specs += [resident(wo.shape), pl.BlockSpec((1, d), const), resident(wg.shape), resident(wu.shape),
                 resident(wd.shape)]
    args = [x, *mixes, wo, gf, wg, wu, wd]
    if final:
        in_specs.append(pl.BlockSpec((1, d), const))
        args.append(gfin)
    return pl.pallas_call(
        functools.partial(_out_ffn_kernel, n_mix=len(mixes), final=final),
        grid=(rows // tm,), in_specs=in_specs,
        out_specs=pl.BlockSpec((tm, d), lambda i: (i, 0)),
        out_shape=jax.ShapeDtypeStruct((rows, d), F32),
        compiler_params=_cparams("parallel"), name="out_ffn")(*args)


def _fox_decode_kernel(pt_ref, qbd_ref, lfn_ref, knew_ref, vnew_ref, sfx_ref, *refs, pps):
    kv_refs = refs[:pps]
    lf_refs = refs[pps:2 * pps]
    o_ref = refs[2 * pps]
    m_ref, l_ref, acc_ref, car_ref = refs[2 * pps + 1:]
    c = pl.program_id(1)
    qbd = qbd_ref[...]

    @pl.when(c == 0)
    def _():
        s_new = jnp.sum(qbd.astype(F32) * knew_ref[...], axis=-1, keepdims=True)
        m_ref[...] = jnp.broadcast_to(s_new, m_ref.shape)
        l_ref[...] = jnp.ones_like(l_ref)
        acc_ref[...] = jnp.broadcast_to(vnew_ref[...], acc_ref.shape)
        car_ref[...] = lfn_ref[...]

    m = m_ref[:, 0:1]
    carry = car_ref[:, 0:1]
    lf_all = jnp.concatenate([lf_refs[i][...] for i in range(pps)], axis=0)
    sfx = sfx_ref[...]
    hi, mid, lo = _split3(lf_all)
    z_all = _dot(hi, sfx) + _dot(mid, sfx) + _dot(lo, sfx)
    tot_all = jnp.sum(lf_all, axis=-1, keepdims=True)
    scores = []
    for i in range(pps):
        rows = slice(FOX_HEADS * i, FOX_HEADS * (i + 1))
        scores.append(_dot(qbd, kv_refs[i][0].astype(BF16)) + (z_all[rows] + carry))
        carry = carry + tot_all[rows]
    mx = scores[0]
    for s in scores[1:]:
        mx = jnp.maximum(mx, s)
    m_new = jnp.maximum(m, jnp.max(mx, axis=-1, keepdims=True))
    alpha = jnp.exp(m - m_new)
    psum = None
    pv = None
    for i in range(pps):
        p = jnp.exp(scores[i] - m_new)
        part = _dot_nt(p.astype(BF16), kv_refs[i][1].astype(BF16))
        psum = p if psum is None else psum + p
        pv = part if pv is None else pv + part
    l = alpha * l_ref[:, 0:1] + jnp.sum(psum, axis=-1, keepdims=True)
    acc = alpha * acc_ref[...] + pv
    m_ref[...] = jnp.broadcast_to(m_new, m_ref.shape)
    l_ref[...] = jnp.broadcast_to(l, l_ref.shape)
    acc_ref[...] = acc
    car_ref[...] = jnp.broadcast_to(carry, car_ref.shape)

    @pl.when(c == pl.num_programs(1) - 1)
    def _():
        o_ref[...] = acc / jnp.maximum(l, TINY)


def _pages_per_step(n_pages, want):
    pps = want
    while n_pages % pps:
        pps //= 2
    return pps


def _fox_decode(page_table, qbd, lfn, knew, vnew, kv_pages, lf_pages):
    nb, npg = page_table.shape
    pps = _pages_per_step(npg, 64)

    def page_map(i):
        return lambda b, c, pt: (pt[b, npg - 1 - (c * pps + i)], 0, 0, 0)

    def lf_map(i):
        return lambda b, c, pt: (pt[b, npg - 1 - (c * pps + i)], 0, 0)

    per_b3 = lambda b, c, pt: (b, 0, 0)
    r = np.arange(PAGE_SIZE)
    sfx = jnp.asarray(r[:, None] > r[None, :], BF16)
    in_specs = [pl.BlockSpec((None, FOX_HEADS, 256), per_b3),
                pl.BlockSpec((None, FOX_HEADS, 128), per_b3),
                pl.BlockSpec((None, 1, 256), per_b3),
                pl.BlockSpec((None, 1, 256), per_b3),
                pl.BlockSpec((PAGE_SIZE, PAGE_SIZE), lambda b, c, pt: (0, 0))]
    in_specs += [pl.BlockSpec((None, 2, 256, PAGE_SIZE), page_map(i)) for i in range(pps)]
    in_specs += [pl.BlockSpec((None, FOX_HEADS, PAGE_SIZE), lf_map(i)) for i in range(pps)]
    grid_spec = pltpu.PrefetchScalarGridSpec(
        num_scalar_prefetch=1, grid=(nb, npg // pps), in_specs=in_specs,
        out_specs=pl.BlockSpec((None, FOX_HEADS, 256), per_b3),
        scratch_shapes=[pltpu.VMEM((FOX_HEADS, 128), F32), pltpu.VMEM((FOX_HEADS, 128), F32),
                        pltpu.VMEM((FOX_HEADS, 256), F32), pltpu.VMEM((FOX_HEADS, 128), F32)])
    return pl.pallas_call(
        functools.partial(_fox_decode_kernel, pps=pps), grid_spec=grid_spec,
        out_shape=jax.ShapeDtypeStruct((nb, FOX_HEADS, 256), F32),
        compiler_params=_cparams("parallel", "arbitrary"), name="fox_decode")(
            page_table, qbd, lfn, knew, vnew, sfx, *([kv_pages] * pps), *([lf_pages] * pps))


def _cmp_decode_kernel(pt_ref, pool_ref, pe_ref, w_ref, *refs, pps):
    x_refs = refs[:pps]
    kc_ref = refs[pps]
    mean = None
    for i in range(pps):
        part = _dot_nt(pool_ref[i], x_refs[i][...].astype(BF16))
        mean = part if mean is None else mean + part
    mean = mean + jnp.mean(pe_ref[...], axis=0, keepdims=True)
    summ = _dot(mean.astype(BF16), w_ref[...])
    half = 2 * pps
    kc_ref[0] = summ[:half]
    kc_ref[1] = summ[half:]


def _cmp_decode(page_table, cmp_pages, pe_rows, wbd):
    nb, npg = page_table.shape
    pps = _pages_per_step(npg, 32)
    per_page = PAGE_SIZE // NSA_CMP_BLOCK
    lane_blk = np.arange(PAGE_SIZE) // NSA_CMP_BLOCK
    pool = np.zeros((pps, per_page * pps, PAGE_SIZE), np.float32)
    for i in range(pps):
        row = (lane_blk % 2) * (2 * pps) + 2 * i + lane_blk // 2
        pool[i, row, np.arange(PAGE_SIZE)] = 1.0 / NSA_CMP_BLOCK
    pool = jnp.asarray(pool, BF16)

    def page_map(i):
        return lambda b, c, pt: (pt[b, c * pps + i], 0, 0)

    in_specs = [pl.BlockSpec(pool.shape, lambda b, c, pt: (0, 0, 0)),
                pl.BlockSpec(pe_rows.shape, lambda b, c, pt: (0, 0)),
                pl.BlockSpec((256, 256), lambda b, c, pt: (0, 0))]
    in_specs += [pl.BlockSpec((None, 256, PAGE_SIZE), page_map(i)) for i in range(pps)]
    n_half = 2 * npg
    grid_spec = pltpu.PrefetchScalarGridSpec(
        num_scalar_prefetch=1, grid=(nb, npg // pps), in_specs=in_specs,
        out_specs=pl.BlockSpec((None, 2, 2 * pps, 256), lambda b, c, pt: (b, 0, c, 0)))
    return pl.pallas_call(
        functools.partial(_cmp_decode_kernel, pps=pps), grid_spec=grid_spec,
        out_shape=jax.ShapeDtypeStruct((nb, 2, n_half, 256), F32),
        compiler_params=_cparams("parallel", "arbitrary"), name="nsa_cmp_decode")(
            page_table, pool, pe_rows, wbd, *([cmp_pages] * pps))


def _nsa_select_kernel(qbd_ref, kc_ref, slope_ref, ocmp_ref, idx_ref, *, past, n_pick):
    n_seq = qbd_ref.shape[0]
    n_half = kc_ref.shape[2]
    nc = 2 * n_half
    lane = _iota((1, nc), 1)
    j_orig = jnp.where(lane < n_half, 2 * lane, 2 * (lane - n_half) + 1)
    dist = past - (NSA_CMP_BLOCK * (j_orig + 1) - 1)
    ok = dist >= 0
    bias = slope_ref[:, 0:1] * dist.astype(F32)
    blk = _iota((1, n_half), 1)
    blk_f = blk.astype(F32)
    slot = _iota((1, NSA_TOPK), 1)
    forced = (blk == 0) | (blk == n_half - 1)
    for u in range(n_seq):
        kc = kc_ref[u].reshape(nc, 256)
        s = jnp.where(ok, _dot_nt(qbd_ref[u], kc[:, :128].astype(BF16)) - bias, NEG_INF)
        m = jnp.max(s, axis=-1, keepdims=True)
        p = jnp.where(ok, jnp.exp(s - m), 0.0)
        p = p / jnp.maximum(jnp.sum(p, axis=-1, keepdims=True), TINY)
        ocmp_ref[u] = _dot(p.astype(BF16), kc[:, 128:].astype(BF16))
        for g in range(NSA_KV_HEADS):
            imp = p[4 * g:4 * g + 1] + p[4 * g + 1:4 * g + 2] + p[4 * g + 2:4 * g + 3] + p[4 * g + 3:4 * g + 4]
            val = jnp.where(forced, FORCED_SCORE, imp[:, :n_half] + imp[:, n_half:])
            picks = jnp.full((1, NSA_TOPK), n_half, jnp.int32)
            for it in range(n_pick):
                mx = jnp.max(val, axis=-1, keepdims=True)
                idx = jnp.min(jnp.where(val == mx, blk_f, float(n_half)), axis=-1, keepdims=True)
                val = jnp.where(blk_f == idx, -2.0, val)
                picks = jnp.where(slot == it, idx.astype(jnp.int32), picks)
            idx_ref[u, pl.ds(g, 1), :] = picks


def _nsa_select(qbd, kc, slopes, past):
    nb, _, n_half, _ = kc.shape
    n_pick = min(NSA_TOPK, n_half + 1) - 1
    n_seq = 8 if nb % 8 == 0 else 1
    return pl.pallas_call(
        functools.partial(_nsa_select_kernel, past=past, n_pick=n_pick), grid=(nb // n_seq,),
        in_specs=[pl.BlockSpec((n_seq, NSA_HEADS, 128), lambda b: (b, 0, 0)),
                  pl.BlockSpec((n_seq, 2, n_half, 256), lambda b: (b, 0, 0, 0)),
                  pl.BlockSpec((NSA_HEADS, 128), lambda b: (0, 0))],
        out_specs=[pl.BlockSpec((n_seq, NSA_HEADS, 128), lambda b: (b, 0, 0)),
                   pl.BlockSpec((n_seq, NSA_KV_HEADS, NSA_TOPK), lambda b: (b, 0, 0))],
        out_shape=[jax.ShapeDtypeStruct((nb, NSA_HEADS, 128), F32),
                   jax.ShapeDtypeStruct((nb, NSA_KV_HEADS, NSA_TOPK), jnp.int32)],
        compiler_params=_cparams("parallel"), name="nsa_select")(qbd, kc, slopes)


def _shift_in(state, new_col):
    n = state.shape[-1]
    lane = _iota(state.shape, 1)
    return jnp.where(lane < n - 1, pltpu.roll(state, n - 1, axis=1), new_col)


def _nsa_decode_kernel(pt_ref, idx_ref, q_ref, new_ref, gate_ref, ocmp_ref, slope_ref, win_ref, *refs,
                       past, n_pick):
    pages = refs[:n_pick]
    o_ref, wout_ref = refs[n_pick:]
    b = pl.program_id(0)
    g = pl.program_id(1)
    q = q_ref[...]
    qf = q.astype(F32)
    new = new_ref[...]
    newb = new.astype(BF16).astype(F32)
    slope = slope_ref[:, 0:1]
    lane = _iota((1, PAGE_SIZE), 1)

    s_new = jnp.sum(qf * newb[0:1], axis=-1, keepdims=True)
    scores, oks = [], []
    for i in range(n_pick):
        blk = idx_ref[b * NSA_KV_HEADS + g, i]
        half = blk % 2
        ok = jnp.right_shift(lane, 6) == half
        pos = blk * NSA_SEL_BLOCK + lane - NSA_SEL_BLOCK * half
        s = _dot(q, pages[i][0].astype(BF16)) - slope * (past - pos).astype(F32)
        scores.append(jnp.where(ok, s, NEG_INF))
        oks.append(ok)
    mx = scores[0]
    for s in scores[1:]:
        mx = jnp.maximum(mx, s)
    m = jnp.maximum(s_new, jnp.max(mx, axis=-1, keepdims=True))
    p_new = jnp.exp(s_new - m)
    psum = None
    pv = None
    for i in range(n_pick):
        p = jnp.where(oks[i], jnp.exp(scores[i] - m), 0.0)
        part = _dot_nt(p.astype(BF16), pages[i][1].astype(BF16))
        psum = p if psum is None else psum + p
        pv = part if pv is None else pv + part
    l = p_new + jnp.sum(psum, axis=-1, keepdims=True)
    o_sel = (p_new * newb[1:2] + pv) / jnp.maximum(l, TINY)

    n_buf = win_ref.shape[-1]
    kT = win_ref[0]
    vT = win_ref[1]
    dist = (n_buf - _iota((1, n_buf), 1)).astype(F32)
    m = jnp.sum(qf * newb[2:3], axis=-1, keepdims=True)
    carry = (m, jnp.ones_like(m), jnp.broadcast_to(newb[3:4], (8, HEAD_DIM)))
    _, l, acc = _softmax_step(carry, _dot(q, kT.astype(BF16)) - slope * dist, vT.astype(BF16))
    o_win = acc / jnp.maximum(l, TINY)

    gate = gate_ref[...]
    o_ref[...] = gate[:, 0:1] * ocmp_ref[...] + gate[:, 1:2] * o_sel + gate[:, 2:3] * o_win
    wout_ref[0] = _shift_in(kT, _row_to_col(new[2:3]))
    wout_ref[1] = _shift_in(vT, _row_to_col(new[3:4]))


def _nsa_decode(page_table, idx, q, new, gate, ocmp, slopes, win_state, sel_pages, past):
    nb, npg = page_table.shape
    n_buf = win_state.shape[-1]
    n_pick = min(NSA_TOPK, 2 * npg + 1) - 1
    idx2 = idx.reshape(nb * NSA_KV_HEADS, NSA_TOPK)

    def page_map(i):
        return lambda b, g, pt, ix: (pt[b, ix[b * NSA_KV_HEADS + g, i] // 2], 0, g, 0, 0)

    per_bg = lambda b, g, pt, ix: (b, g, 0, 0)
    in_specs = [pl.BlockSpec((None, None, 8, HEAD_DIM), per_bg),
                pl.BlockSpec((None, None, 4, HEAD_DIM), per_bg),
                pl.BlockSpec((None, None, 8, 128), per_bg),
                pl.BlockSpec((None, None, 8, HEAD_DIM), per_bg),
                pl.BlockSpec((None, 8, 128), lambda b, g, pt, ix: (g, 0, 0)),
                pl.BlockSpec((None, 2, None, HEAD_DIM, n_buf), lambda b, g, pt, ix: (b, 0, g, 0, 0))]
    in_specs += [pl.BlockSpec((None, 2, None, HEAD_DIM, PAGE_SIZE), page_map(i)) for i in range(n_pick)]
    grid_spec = pltpu.PrefetchScalarGridSpec(
        num_scalar_prefetch=2, grid=(nb, NSA_KV_HEADS), in_specs=in_specs,
        out_specs=[pl.BlockSpec((None, None, 8, HEAD_DIM), per_bg),
                   pl.BlockSpec((None, 2, None, HEAD_DIM, n_buf), lambda b, g, pt, ix: (b, 0, g, 0, 0))])
    return pl.pallas_call(
        functools.partial(_nsa_decode_kernel, past=past, n_pick=n_pick), grid_spec=grid_spec,
        out_shape=[jax.ShapeDtypeStruct((nb, NSA_KV_HEADS, 8, HEAD_DIM), F32),
                   jax.ShapeDtypeStruct(win_state.shape, F32)],
        compiler_params=_cparams("parallel", "arbitrary"), name="nsa_decode")(
            page_table, idx2, q, new, gate, ocmp, slopes, win_state, *([sel_pages] * n_pick))


def _swa_decode_kernel(qbd_ref, new_ref, sink_ref, slope_ref, st_ref, o_ref, sout_ref):
    n_buf = st_ref.shape[-1]
    qbd = qbd_ref[...]
    new = new_ref[...]
    newb = new.astype(BF16).astype(F32)
    kT = st_ref[0].reshape(128, n_buf)
    vT = st_ref[1].reshape(128, n_buf)
    dist = (n_buf - _iota((1, n_buf), 1)).astype(F32)
    s = _dot(qbd, kT.astype(BF16)) - slope_ref[:, 0:1] * dist
    s_new = jnp.sum(qbd.astype(F32) * newb[:, :128], axis=-1, keepdims=True)
    sink = sink_ref[:, 0:1]
    m = jnp.maximum(jnp.maximum(jnp.max(s, axis=-1, keepdims=True), s_new), sink)
    p = jnp.exp(s - m)
    p_new = jnp.exp(s_new - m)
    den = jnp.sum(p, axis=-1, keepdims=True) + p_new + jnp.exp(sink - m)
    acc = _dot_nt(p.astype(BF16), vT.astype(BF16)) + p_new.astype(BF16).astype(F32) * newb[:, 128:]
    o_ref[...] = acc / jnp.maximum(den, TINY)
    sout_ref[0] = _shift_in(kT, _row_to_col(new[:, :128])).reshape(st_ref.shape[1:])
    sout_ref[1] = _shift_in(vT, _row_to_col(new[:, 128:])).reshape(st_ref.shape[1:])


def _swa_decode(qbd, new, sinks, slopes, state):
    nb = qbd.shape[0]
    n_buf = state.shape[-1]
    st_spec = pl.BlockSpec((None, 2, SWA_KV_HEADS, HEAD_DIM, n_buf), lambda b: (b, 0, 0, 0, 0))
    return pl.pallas_call(
        _swa_decode_kernel, grid=(nb,),
        in_specs=[pl.BlockSpec((None, SWA_HEADS, 128), lambda b: (b, 0, 0)),
                  pl.BlockSpec((None, 1, 256), lambda b: (b, 0, 0)),
                  pl.BlockSpec((SWA_HEADS, 128), lambda b: (0, 0)),
                  pl.BlockSpec((SWA_HEADS, 128), lambda b: (0, 0)),
                  st_spec],
        out_specs=[pl.BlockSpec((None, SWA_HEADS, 128), lambda b: (b, 0, 0)), st_spec],
        out_shape=[jax.ShapeDtypeStruct((nb, SWA_HEADS, 128), F32),
                   jax.ShapeDtypeStruct(state.shape, F32)],
        compiler_params=_cparams("parallel"), name="swa_decode")(qbd, new, sinks, slopes, state)


def _block_diag_q(q, n_groups, per_group):
    nb = q.shape[0]
    q5 = q.reshape(nb, n_groups, per_group, 1, HEAD_DIM)
    eye = jnp.eye(n_groups, dtype=q.dtype).reshape(1, n_groups, 1, n_groups, 1)
    return (q5 * eye).reshape(nb, n_groups * per_group, n_groups * HEAD_DIM)


def _block_diag_take(o, n_groups, per_group):
    nb = o.shape[0]
    o5 = o.reshape(nb, n_groups, per_group, n_groups, HEAD_DIM)
    picked = jnp.stack([o5[:, g, :, g, :] for g in range(n_groups)], axis=1)
    return picked.reshape(nb, n_groups * per_group * HEAD_DIM)


def _t_last_view(x):
    nd = x.ndim
    return jnp.transpose(x, (0,) + tuple(range(2, nd)) + (1,))


def _t_first_view(x):
    nd = x.ndim
    return jnp.transpose(x, (0, nd - 1) + tuple(range(1, nd - 1)))


def kernel(x_prompt, x_sample, cache_fox_kv, cache_fox_logf, cache_nsa_cmp_kv, cache_nsa_sel_kv, state_nsa_win,
           state_swa, page_table, even_w_in, even_b_forget, even_pe_cmp, even_w_cmp, even_w_out, odd_w_in,
           odd_sinks, odd_w_out, norm_mix, norm_ffn, ffn_w_gate, ffn_w_up, ffn_w_down, norm_final):
    b, t, d = x_prompt.shape
    nb = x_sample.shape[0]
    npg = page_table.shape[1]
    past = npg * PAGE_SIZE
    assert norm_mix.shape[0] == 2 and x_sample.shape[1] == 1
    assert past >= NSA_WINDOW and t % 512 == 0

    w = even_w_in[0]
    fq, fkv, fl = w[:, 0:512], w[:, 512:1024], w[:, 1024:1032]
    nq, nkv, ng = w[:, 1032:1544], w[:, 1544:2312], w[:, 2312:2336]

    def pad_heads(wq):
        wq3 = wq.reshape(d, -1, HEAD_DIM)
        return jnp.concatenate([wq3, jnp.zeros_like(wq3)], axis=2).reshape(d, -1)

    e_wn = jnp.concatenate([pad_heads(fq * ATTN_SCALE), pad_heads(nq * ATTN_SCALE), fl, ng,
                            jnp.zeros((d, 96), F32)], axis=1).astype(BF16)
    e_wt = jnp.concatenate([fkv, nkv, fl], axis=1).T.astype(BF16)
    e_wr = jnp.concatenate([fkv, nkv], axis=1).astype(BF16)
    bsm = jnp.zeros((1, 128), F32).at[0, :FOX_HEADS].set(even_b_forget[0])
    bcol = even_b_forget[0].reshape(FOX_HEADS, 1)
    w_cmp = even_w_cmp[0].reshape(4, HEAD_DIM, HEAD_DIM)
    wbd = jnp.zeros((256, 256), F32)
    for i in range(4):
        wbd = wbd.at[i * 64:(i + 1) * 64, i * 64:(i + 1) * 64].set(w_cmp[i])
    wbd_b = wbd.astype(BF16)
    w_aug = jnp.concatenate([wbd.reshape(256, 4, HEAD_DIM), jnp.zeros((256, 4, HEAD_DIM), F32)], axis=2)
    w_augT = w_aug.reshape(256, 512).T.astype(BF16)
    pe_rows = even_pe_cmp[0].reshape(NSA_CMP_BLOCK, 256)
    o_w = odd_w_in[0]
    o_wn = (o_w[:, :1024] * ATTN_SCALE).astype(BF16)
    o_wt = o_w[:, 1024:].T.astype(BF16)
    o_wr = o_w[:, 1024:].astype(BF16)
    e_wo = even_w_out[0].astype(BF16)
    o_wo = odd_w_out[0].astype(BF16)
    wg = ffn_w_gate.astype(BF16)
    wu = ffn_w_up.astype(BF16)
    wd = ffn_w_down.astype(BF16)
    g_mix = norm_mix.reshape(2, 1, d)
    g_ffn = norm_ffn.reshape(2, 1, d)
    g_fin = norm_final.reshape(1, d)
    sinks_row = jnp.zeros((1, 128), F32).at[0, :SWA_HEADS].set(odd_sinks[0])
    sinks_col = jnp.broadcast_to(odd_sinks[0].reshape(SWA_HEADS, 1), (SWA_HEADS, 128))
    nsa_slopes8 = jnp.broadcast_to(jnp.asarray(NSA_SLOPES, F32).reshape(NSA_HEADS, 1), (NSA_HEADS, 128))
    nsa_slopes_g = jnp.concatenate([nsa_slopes8.reshape(2, 4, 128), jnp.zeros((2, 4, 128), F32)], axis=1)
    swa_slopes16 = jnp.broadcast_to(jnp.asarray(SWA_SLOPES, F32).reshape(SWA_HEADS, 1), (SWA_HEADS, 128))

    fox_pages = _t_last_view(cache_fox_kv[0]).reshape(-1, 2, 256, PAGE_SIZE)
    lf_pages = _t_last_view(cache_fox_logf[0])
    cmp_pages = _t_last_view(cache_nsa_cmp_kv[0]).reshape(-1, 256, PAGE_SIZE)
    sel_pages = _t_last_view(cache_nsa_sel_kv[0])
    win_state = _t_last_view(state_nsa_win[0])
    swa_state = _t_last_view(state_swa[0])

    (q_p, sm_p, fkvT, cmpT, selT, winT, logfT, fkvTb, selTb, winTb, kselA, kwinA) = _even_proj(
        x_prompt, g_mix[0], e_wn, e_wt, bsm, bcol)
    c, kfoxA = _cumsum(logfT, sm_p, fkvTb)
    o_fox = _fox_attn(q_p, c, kfoxA, fkvTb)
    kcT, kc = _cmp_prompt(cmpT, pe_rows.T, w_augT)
    o_nsa = _nsa_attn(q_p, sm_p, kcT, kc, kselA, selTb, kwinA, winTb)
    xp = _out_ffn(x_prompt.reshape(b * t, d), [o_fox.reshape(b * t, 512), o_nsa.reshape(b * t, 512)],
                  e_wo, g_ffn[0], wg[0], wu[0], wd[0])

    xs0 = x_sample.reshape(1, nb, d)
    (q_s, sm_s, fkvT_s, cmpT_s, selT_s, winT_s, logfT_s, _, _, _, _, _, rows_s) = _even_proj(
        xs0, g_mix[0], e_wn, e_wt, bsm, bcol, e_wr)
    q_s = q_s[0].reshape(nb, FOX_HEADS + NSA_HEADS, 128)[:, :, :HEAD_DIM].reshape(nb, 1024)
    sm_s, rows_s = sm_s[0], rows_s[0]
    rows_sb = rows_s.astype(BF16).astype(F32)
    qbd_f = _block_diag_q(q_s[:, :512], FOX_KV_HEADS, 2)
    lfn = jnp.broadcast_to(sm_s[:, :FOX_HEADS, None], (nb, FOX_HEADS, 128))
    o_fox_s = _fox_decode(page_table, qbd_f, lfn, rows_sb[:, None, 0:256], rows_sb[:, None, 256:512],
                          fox_pages, lf_pages)
    o_fox_s = _block_diag_take(o_fox_s, FOX_KV_HEADS, 2)
    kc_s = _cmp_decode(page_table, cmp_pages, pe_rows, wbd_b)
    qbd_n = _block_diag_q(q_s[:, 512:], NSA_KV_HEADS, 4)
    ocmp_s, idx_s = _nsa_select(qbd_n, kc_s, nsa_slopes8, past)
    ocmp_s = _block_diag_take(ocmp_s, NSA_KV_HEADS, 4).reshape(nb, NSA_KV_HEADS, 4, HEAD_DIM)
    pad4 = jnp.zeros((nb, NSA_KV_HEADS, 4, HEAD_DIM), F32)
    ocmp_s = jnp.concatenate([ocmp_s, pad4], axis=2)
    qn_s = jnp.concatenate([q_s[:, 512:].reshape(nb, NSA_KV_HEADS, 4, HEAD_DIM), pad4.astype(BF16)], axis=2)
    nkv_s = rows_s[:, 512:].reshape(nb, 6, NSA_KV_HEADS, HEAD_DIM)
    new_s = jnp.transpose(nkv_s[:, 2:6], (0, 2, 1, 3))
    gate_s = sm_s[:, FOX_HEADS:FOX_HEADS + 24].reshape(nb, NSA_KV_HEADS, 4, 3)
    gate_s = jnp.zeros((nb, NSA_KV_HEADS, 8, 128), F32).at[:, :, :4, :3].set(gate_s)
    o_nsa_s, win_new = _nsa_decode(page_table, idx_s, qn_s, new_s, gate_s, ocmp_s, nsa_slopes_g, win_state,
                                   sel_pages, past)
    o_nsa_s = o_nsa_s[:, :, :4].reshape(nb, 512)
    xs = _out_ffn(x_sample.reshape(nb, d), [o_fox_s.astype(BF16), o_nsa_s.astype(BF16)],
                  e_wo, g_ffn[0], wg[0], wu[0], wd[0])

    q1, kvT1, kvT1b = _odd_proj(xp.reshape(b, t, d), g_mix[1], o_wn, o_wt)
    o_swa = _swa_attn(q1, kvT1b, sinks_row)
    y_prompt = _out_ffn(xp, [o_swa.reshape(b * t, 1024)], o_wo, g_ffn[1], wg[1], wu[1], wd[1], g_fin)

    q1s, kvT1s, _, rows1s = _odd_proj(xs.reshape(1, nb, d), g_mix[1], o_wn, o_wt, o_wr)
    qbd_s = _block_diag_q(q1s[0], SWA_KV_HEADS, 8)
    o_swa_s, swa_new = _swa_decode(qbd_s, rows1s[0][:, None, :], sinks_col, swa_slopes16, swa_state)
    o_swa_s = _block_diag_take(o_swa_s, SWA_KV_HEADS, 8)
    y_sample = _out_ffn(xs, [o_swa_s.astype(BF16)], o_wo, g_ffn[1], wg[1], wu[1], wd[1], g_fin)

    def rows_first(xT, *feat):
        n, _, rows = xT.shape
        return _t_first_view(xT.reshape((n,) + feat + (rows,)))[None]

    n_win = min(NSA_WINDOW, t)
    n_swa = min(SWA_WINDOW, t)
    return (y_prompt.reshape(b, t, d), y_sample.reshape(nb, 1, d),
            rows_first(fkvT, 2, FOX_KV_HEADS, HEAD_DIM),
            jnp.transpose(rows_first(fkvT_s, 2, FOX_KV_HEADS, HEAD_DIM), (0, 2, 1, 3, 4, 5)),
            rows_first(logfT, FOX_HEADS),
            jnp.transpose(rows_first(logfT_s, FOX_HEADS), (0, 2, 1, 3)),
            rows_first(cmpT, 2, NSA_KV_HEADS, HEAD_DIM),
            jnp.transpose(rows_first(cmpT_s, 2, NSA_KV_HEADS, HEAD_DIM), (0, 2, 1, 3, 4, 5)),
            rows_first(selT, 2, NSA_KV_HEADS, HEAD_DIM),
            jnp.transpose(rows_first(selT_s, 2, NSA_KV_HEADS, HEAD_DIM), (0, 2, 1, 3, 4, 5)),
            rows_first(winT[:, :, t - n_win:], 2, NSA_KV_HEADS, HEAD_DIM),
            _t_first_view(win_new)[None],
            rows_first(kvT1[:, :, t - n_swa:], 2, SWA_KV_HEADS, HEAD_DIM),
            _t_first_view(swa_new)[None])
```
